```python
import math
import jax, jax.numpy as jnp
from jax import lax
import numpy as np

D_MODEL = 2048
BATCH = 4
SEQ = 4096
DEPTH = 1
DEC_BATCH = 4
DEC_SEQ = 2048
PAST_LEN = 128

MIX_WIDTH = D_MODEL
HEAD_DIM = 128
DIFF_WIDTH = MIX_WIDTH // 2
RET_WIDTH = MIX_WIDTH - DIFF_WIDTH
H_DIFF = DIFF_WIDTH // HEAD_DIM
H_RET = RET_WIDTH // HEAD_DIM
DIFF_QK_DIM = HEAD_DIM // 2
D_FF = ((8 * D_MODEL // 3 + 255) // 256) * 256
IN_SIZES = (H_DIFF * 2 * DIFF_QK_DIM, H_DIFF * 2 * DIFF_QK_DIM, H_DIFF * HEAD_DIM,
            H_RET * HEAD_DIM, H_RET * HEAD_DIM, H_RET * HEAD_DIM, RET_WIDTH)
IN_WIDTH = sum(IN_SIZES)
N_BUCKETS = 32
REL_MAX_DIST = 128
Q_BLOCK = 128
RET_CHUNK = 128
ROPE_BASE = 10000.0
N_MOD = 9
EPS = 1e-6

kernel_name = "hybrid_diffattn_retention_macaron_encoder"


def _rms(x, g):
    xf = x.astype(jnp.float32)
    y = xf * lax.rsqrt(jnp.mean(xf * xf, axis=-1, keepdims=True) + EPS)
    return (y * g.astype(jnp.float32)).astype(x.dtype)


def _modulate(h, shift, scale):
    return h * (1.0 + scale[:, None, :]) + shift[:, None, :]


def _swiglu(h, w13, w2):
    a = h @ w13
    gate, up = jnp.split(a, 2, axis=-1)
    return (jax.nn.silu(gate) * up) @ w2


def _t5_bucket(rel):
    half = N_BUCKETS // 2
    max_exact = half // 2
    ret = jnp.where(rel > 0, half, 0)
    n = jnp.abs(rel)
    nf = jnp.maximum(n, 1).astype(jnp.float32)
    large = max_exact + (jnp.log(nf / max_exact) / math.log(REL_MAX_DIST / max_exact)
                         * (half - max_exact)).astype(jnp.int32)
    large = jnp.minimum(large, half - 1)
    return ret + jnp.where(n < max_exact, n, large)


def _rotary(x, pos):
    d = x.shape[-1]
    inv = ROPE_BASE ** (-jnp.arange(0, d, 2, dtype=jnp.float32) / d)
    ang = pos[:, None] * inv[None, :]
    cos = jnp.cos(ang)[None, :, None, :]
    sin = jnp.sin(ang)[None, :, None, :]
    x1, x2 = x[..., : d // 2], x[..., d // 2:]
    return jnp.concatenate([x1 * cos - x2 * sin, x1 * sin + x2 * cos], axis=-1)


def _diff_attention(q, k, v, rel_bias, lam):
    B, S, H, _, dq = q.shape
    dv = v.shape[-1]
    scale = dq ** -0.5
    k1, k2 = k[..., 0, :], k[..., 1, :]
    kpos = jnp.arange(S, dtype=jnp.int32)
    nb = S // Q_BLOCK
    qb = q.reshape(B, nb, Q_BLOCK, H, 2, dq).transpose(1, 0, 3, 2, 4, 5)
    starts = jnp.arange(nb, dtype=jnp.int32) * Q_BLOCK

    def block(args):
        qi, s0 = args
        qpos = s0 + jnp.arange(Q_BLOCK, dtype=jnp.int32)
        bucket = _t5_bucket(kpos[None, :] - qpos[:, None])
        bias = jnp.transpose(rel_bias[bucket], (2, 0, 1)).astype(jnp.float32)
        s1 = jnp.einsum('bhqd,bkhd->bhqk', qi[..., 0, :], k1).astype(jnp.float32) * scale + bias
        s2 = jnp.einsum('bhqd,bkhd->bhqk', qi[..., 1, :], k2).astype(jnp.float32) * scale + bias
        p = jax.nn.softmax(s1, axis=-1) - lam * jax.nn.softmax(s2, axis=-1)
        return jnp.einsum('bhqk,bkhv->bqhv', p.astype(v.dtype), v)

    out = lax.map(block, (qb, starts))
    return out.transpose(1, 0, 2, 3, 4).reshape(B, S, H, dv)


def _retention_dir(q, k, v, log_gamma, include_diag):
    B, S, H, d = q.shape
    dv = v.shape[-1]
    C = RET_CHUNK
    n = S // C

    def chunks(t):
        return t.reshape(B, n, C, H, t.shape[-1]).transpose(1, 0, 3, 2, 4)

    idx = jnp.arange(C, dtype=jnp.float32)
    diff = idx[:, None] - idx[None, :]
    mask = (diff >= 0) if include_diag else (diff > 0)
    decay = jnp.where(mask[None], jnp.exp(log_gamma[:, None, None] * jnp.maximum(diff, 0.0)[None]), 0.0)
    xi = jnp.exp(log_gamma[:, None] * (idx + 1.0)[None])[..., None]
    zeta = jnp.exp(log_gamma[:, None] * (C - 1.0 - idx)[None])[..., None]
    g_c = jnp.exp(log_gamma * C)[:, None, None]

    def step(R, qkv):
        qc, kc, vc = qkv
        inner = jnp.einsum('bhnd,bhmd->bhnm', qc, kc) * decay
        o = jnp.einsum('bhnm,bhmv->bhnv', inner, vc) + jnp.einsum('bhnd,bhdv->bhnv', qc * xi, R)
        R = g_c * R + jnp.einsum('bhmd,bhmv->bhdv', kc * zeta, vc)
        return R, o

    R0 = jnp.zeros((B, H, d, dv), jnp.float32)
    _, o = lax.scan(step, R0, (chunks(q), chunks(k), chunks(v)))
    return o.transpose(1, 0, 3, 2, 4).reshape(B, S, H, dv)


def _mixer(h, layer_idx, w_in, lq1, lk1, lq2, lk2, diff_head_g, rel_bias,
           decay_f, decay_b, ret_head_g, w_out):
    B, S, _ = h.shape
    proj = h @ w_in
    split_pts = [int(p) for p in np.cumsum(IN_SIZES)[:-1]]
    dq, dk, dv, rq, rk, rv, rg = jnp.split(proj, split_pts, axis=-1)

    dq = dq.reshape(B, S, H_DIFF, 2, DIFF_QK_DIM)
    dk = dk.reshape(B, S, H_DIFF, 2, DIFF_QK_DIM)
    dv = dv.reshape(B, S, H_DIFF, HEAD_DIM)
    lam_init = 0.8 - 0.6 * math.exp(-0.3 * layer_idx)
    lam = (jnp.exp(jnp.sum(lq1.astype(jnp.float32) * lk1.astype(jnp.float32)))
           - jnp.exp(jnp.sum(lq2.astype(jnp.float32) * lk2.astype(jnp.float32))) + lam_init)
    d_out = _diff_attention(dq, dk, dv, rel_bias, lam)
    d_out = (_rms(d_out, diff_head_g) * (1.0 - lam_init)).reshape(B, S, DIFF_WIDTH)

    pos = jnp.arange(S, dtype=jnp.float32)
    rq = _rotary(rq.reshape(B, S, H_RET, HEAD_DIM).astype(jnp.float32), pos)
    rk = _rotary(rk.reshape(B, S, H_RET, HEAD_DIM).astype(jnp.float32), pos) * (HEAD_DIM ** -0.5)
    rv = rv.reshape(B, S, H_RET, HEAD_DIM).astype(jnp.float32)
    lg_f = jax.nn.log_sigmoid(decay_f.astype(jnp.float32))
    lg_b = jax.nn.log_sigmoid(decay_b.astype(jnp.float32))
    o_f = _retention_dir(rq, rk, rv, lg_f, True)
    o_b = jnp.flip(_retention_dir(jnp.flip(rq, 1), jnp.flip(rk, 1), jnp.flip(rv, 1), lg_b, False), 1)
    r_out = _rms(o_f + o_b, ret_head_g).astype(h.dtype).reshape(B, S, RET_WIDTH) * jax.nn.silu(rg)

    return jnp.concatenate([d_out, r_out], axis=-1) @ w_out


def _trunk(x, c, ada_w, ada_b, ffn1_norm_g, ffn1_w13, ffn1_w2, mix_norm_g, w_in,
           diff_lambda_q1, diff_lambda_k1, diff_lambda_q2, diff_lambda_k2, diff_head_g,
           rel_bias, ret_decay_fwd, ret_decay_bwd, ret_head_g, w_out,
           ffn2_norm_g, ffn2_w13, ffn2_w2, final_norm_g):
    for l in range(DEPTH):
        mod = jax.nn.silu(c) @ ada_w[l] + ada_b[l]
        sh1, sc1, g1, shm, scm, gm, sh2, sc2, g2 = jnp.split(mod, N_MOD, axis=-1)
        h = _modulate(_rms(x, ffn1_norm_g[l]), sh1, sc1)
        x = x + 0.5 * g1[:, None, :] * _swiglu(h, ffn1_w13[l], ffn1_w2[l])
        h = _modulate(_rms(x, mix_norm_g[l]), shm, scm)
        x = x + gm[:, None, :] * _mixer(h, l, w_in[l], diff_lambda_q1[l], diff_lambda_k1[l],
                                         diff_lambda_q2[l], diff_lambda_k2[l], diff_head_g[l],
                                         rel_bias, ret_decay_fwd[l], ret_decay_bwd[l],
                                         ret_head_g[l], w_out[l])
        h = _modulate(_rms(x, ffn2_norm_g[l]), sh2, sc2)
        x = x + 0.5 * g2[:, None, :] * _swiglu(h, ffn2_w13[l], ffn2_w2[l])
    return _rms(x, final_norm_g)


def setup_inputs(seed: int = 0) -> dict:
    key = jax.random.key(seed)
    ks = jax.random.split(key, 26)
    f32 = jnp.float32
    nrm = lambda k, shape, s: jax.random.normal(k, shape, f32) * s
    gain = lambda k, shape: 1.0 + 0.05 * jax.random.normal(k, shape, f32)
    base_logit = jnp.log(2.0 ** (5.0 + jnp.arange(H_RET, dtype=f32)) - 1.0)
    return {
        "x_prompt": nrm(ks[0], (BATCH, SEQ, D_MODEL), 1.0),
        "x_sample": nrm(ks[1], (DEC_BATCH, DEC_SEQ, D_MODEL), 1.0),
        "c_prompt": nrm(ks[2], (BATCH, D_MODEL), 1.0),
        "c_sample": nrm(ks[3], (DEC_BATCH, D_MODEL), 1.0),
        "ada_w": nrm(ks[4], (DEPTH, D_MODEL, N_MOD * D_MODEL), D_MODEL ** -0.5),
        "ada_b": nrm(ks[5], (DEPTH, N_MOD * D_MODEL), 0.02),
        "ffn1_norm_g": gain(ks[6], (DEPTH, D_MODEL)),
        "ffn1_w13": nrm(ks[7], (DEPTH, D_MODEL, 2 * D_FF), D_MODEL ** -0.5),
        "ffn1_w2": nrm(ks[8], (DEPTH, D_FF, D_MODEL), D_FF ** -0.5),
        "mix_norm_g": gain(ks[9], (DEPTH, D_MODEL)),
        "w_in": nrm(ks[10], (DEPTH, D_MODEL, IN_WIDTH), D_MODEL ** -0.5),
        "diff_lambda_q1": nrm(ks[11], (DEPTH, DIFF_QK_DIM), 0.1),
        "diff_lambda_k1": nrm(ks[12], (DEPTH, DIFF_QK_DIM), 0.1),
        "diff_lambda_q2": nrm(ks[13], (DEPTH, DIFF_QK_DIM), 0.1),
        "diff_lambda_k2": nrm(ks[14], (DEPTH, DIFF_QK_DIM), 0.1),
        "diff_head_g": gain(ks[15], (DEPTH, HEAD_DIM)),
        "rel_bias": nrm(ks[16], (N_BUCKETS, H_DIFF), 0.5),
        "ret_decay_fwd": base_logit[None, :] + nrm(ks[17], (DEPTH, H_RET), 0.05),
        "ret_decay_bwd": base_logit[None, :] + nrm(ks[18], (DEPTH, H_RET), 0.05),
        "ret_head_g": gain(ks[19], (DEPTH, HEAD_DIM)),
        "w_out": nrm(ks[20], (DEPTH, MIX_WIDTH, D_MODEL), MIX_WIDTH ** -0.5),
        "ffn2_norm_g": gain(ks[21], (DEPTH, D_MODEL)),
        "ffn2_w13": nrm(ks[22], (DEPTH, D_MODEL, 2 * D_FF), D_MODEL ** -0.5),
        "ffn2_w2": nrm(ks[23], (DEPTH, D_FF, D_MODEL), D_FF ** -0.5),
        "final_norm_g": gain(ks[24], (D_MODEL,)),
    }


def reference(x_prompt, x_sample, c_prompt, c_sample, ada_w, ada_b, ffn1_norm_g, ffn1_w13, ffn1_w2,
              mix_norm_g, w_in, diff_lambda_q1, diff_lambda_k1, diff_lambda_q2, diff_lambda_k2,
              diff_head_g, rel_bias, ret_decay_fwd, ret_decay_bwd, ret_head_g, w_out,
              ffn2_norm_g, ffn2_w13, ffn2_w2, final_norm_g):
    y_prompt = _trunk(x_prompt, c_prompt, ada_w, ada_b, ffn1_norm_g, ffn1_w13, ffn1_w2, mix_norm_g, w_in,
                      diff_lambda_q1, diff_lambda_k1, diff_lambda_q2, diff_lambda_k2, diff_head_g,
                      rel_bias, ret_decay_fwd, ret_decay_bwd, ret_head_g, w_out,
                      ffn2_norm_g, ffn2_w13, ffn2_w2, final_norm_g)
    y_sample = _trunk(x_sample, c_sample, ada_w, ada_b, ffn1_norm_g, ffn1_w13, ffn1_w2, mix_norm_g, w_in,
                      diff_lambda_q1, diff_lambda_k1, diff_lambda_q2, diff_lambda_k2, diff_head_g,
                      rel_bias, ret_decay_fwd, ret_decay_bwd, ret_head_g, w_out,
                      ffn2_norm_g, ffn2_w13, ffn2_w2, final_norm_g)
    return (y_prompt, y_sample)
```

```python
import functools
import math

import jax
import jax.numpy as jnp
from jax import lax
from jax.experimental import pallas as pl
from jax.experimental.pallas import tpu as pltpu

F32 = jnp.float32
BF16 = jnp.bfloat16

HEAD_DIM = 128
DIFF_QK_DIM = HEAD_DIM // 2
N_BUCKETS = 32
REL_MAX_DIST = 128
ROPE_BASE = 10000.0
N_MOD = 9
EPS = 1e-6
LOG2E = math.log2(math.e)
N_PROJ = 7

V7X_VMEM_BYTES = 64 * 1024 * 1024
VMEM_LIMIT = 56 * 1024 * 1024

FFN_ROW_TILE = 512
FFN_HID_TILE = 512
PROJ_ROW_TILE = 512
ATTN_TILE = 512
RET_CHUNK = 256
ADA_COL_TILE = 1024


def _params(sem):
    return pltpu.CompilerParams(dimension_semantics=sem, vmem_limit_bytes=VMEM_LIMIT)


def _rms_rows(x, g):
    return x * lax.rsqrt(jnp.mean(x * x, axis=-1, keepdims=True) + EPS) * g


def _silu(x):
    return x * jax.nn.sigmoid(x)


def _ada_kernel(c_ref, w_ref, b_ref, o_ref):
    a = _silu(c_ref[...]).astype(BF16)
    o_ref[...] = jnp.dot(a, w_ref[...].astype(BF16), preferred_element_type=F32) + b_ref[...]


def _ada_mod(c, w, b):
    nb, d = c.shape
    n = w.shape[1]
    tn = ADA_COL_TILE
    return pl.pallas_call(
        _ada_kernel,
        out_shape=jax.ShapeDtypeStruct((nb, n), F32),
        grid=(n // tn,),
        in_specs=[pl.BlockSpec((nb, d), lambda j: (0, 0)),
                  pl.BlockSpec((d, tn), lambda j: (0, j)),
                  pl.BlockSpec((1, tn), lambda j: (0, j))],
        out_specs=pl.BlockSpec((nb, tn), lambda j: (0, j)),
        compiler_params=_params(("parallel",)),
        name="ada_mod",
    )(c, w, b.reshape(1, n))


def _ffn_kernel(x_ref, sh_ref, sc_ref, gt_ref, ng_ref, fg_ref, w1_ref, w3_ref, w2_ref,
                o_ref, h_ref, *, final_norm):
    j = pl.program_id(1)

    @pl.when(j == 0)
    def _():
        h = _rms_rows(x_ref[...], ng_ref[...]) * (1.0 + sc_ref[...]) + sh_ref[...]
        h_ref[...] = h.astype(BF16)

    h = h_ref[...]
    g = jnp.dot(h, w1_ref[...], preferred_element_type=F32)
    u = jnp.dot(h, w3_ref[...], preferred_element_type=F32)
    part = jnp.dot((_silu(g) * u).astype(BF16), w2_ref[...], preferred_element_type=F32)

    @pl.when(j == 0)
    def _():
        o_ref[...] = part

    @pl.when(j > 0)
    def _():
        o_ref[...] += part

    @pl.when(j == pl.num_programs(1) - 1)
    def _():
        y = x_ref[...] + 0.5 * gt_ref[...] * o_ref[...]
        if final_norm:
            y = _rms_rows(y, fg_ref[...])
        o_ref[...] = y


def _ffn(x, shift, scale, gate, norm_g, final_g, w13, w2, *, seq_len, final_norm):
    n, d = x.shape
    f = w2.shape[0]
    tm, tf = FFN_ROW_TILE, FFN_HID_TILE
    nf = f // tf
    seq = lambda i, j: ((i * tm) // seq_len, 0, 0)
    row = lambda i, j: (i, 0)
    const = lambda i, j: (0, 0)
    return pl.pallas_call(
        functools.partial(_ffn_kernel, final_norm=final_norm),
        out_shape=jax.ShapeDtypeStruct((n, d), F32),
        grid=(n // tm, nf),
        in_specs=[pl.BlockSpec((tm, d), row),
                  pl.BlockSpec((None, 1, d), seq),
                  pl.BlockSpec((None, 1, d), seq),
                  pl.BlockSpec((None, 1, d), seq),
                  pl.BlockSpec((1, d), const),
                  pl.BlockSpec((1, d), const),
                  pl.BlockSpec((d, tf), lambda i, j: (0, j)),
                  pl.BlockSpec((d, tf), lambda i, j: (0, nf + j)),
                  pl.BlockSpec((tf, d), lambda i, j: (j, 0))],
        out_specs=pl.BlockSpec((tm, d), row),
        scratch_shapes=[pltpu.VMEM((tm, d), BF16)],
        compiler_params=_params(("parallel", "arbitrary")),
        name="ffn_final" if final_norm else "ffn",
    )(x, shift, scale, gate, norm_g, final_g, w13, w13, w2)


def _inproj_kernel(x_ref, sh_ref, sc_ref, ng_ref, cq_ref, sq_ref, w_ref, o_ref, h_ref):
    j = pl.program_id(1)

    @pl.when(j == 0)
    def _():
        h = _rms_rows(x_ref[...], ng_ref[...]) * (1.0 + sc_ref[...]) + sh_ref[...]
        h_ref[...] = h.astype(BF16)

    res = jnp.dot(h_ref[...], w_ref[...], preferred_element_type=F32)

    def rotary(scale):
        cos, sin = cq_ref[...], sq_ref[...]
        for hd in range(res.shape[1] // HEAD_DIM):
            xs = res[:, hd * HEAD_DIM:(hd + 1) * HEAD_DIM]
            rot = pltpu.roll(xs, HEAD_DIM // 2, 1)
            o_ref[:, hd * HEAD_DIM:(hd + 1) * HEAD_DIM] = ((xs * cos + rot * sin) * scale).astype(BF16)

    @pl.when(j == 0)
    def _():
        o_ref[...] = (res * (DIFF_QK_DIM ** -0.5 * LOG2E)).astype(BF16)

    @pl.when(j == 3)
    def _():
        rotary(1.0)

    @pl.when(j == 4)
    def _():
        rotary(HEAD_DIM ** -0.5)

    @pl.when((j != 0) & (j != 3) & (j != 4))
    def _():
        o_ref[...] = res.astype(BF16)


def _inproj(x, shift, scale, norm_g, cos_t, sin_t, w_in, *, seq_len):
    n, d = x.shape
    gw = w_in.shape[1] // N_PROJ
    tm = PROJ_ROW_TILE
    spt = seq_len // tm
    seq = lambda i, j: ((i * tm) // seq_len, 0, 0)
    return pl.pallas_call(
        _inproj_kernel,
        out_shape=jax.ShapeDtypeStruct((N_PROJ, n, gw), BF16),
        grid=(n // tm, N_PROJ),
        in_specs=[pl.BlockSpec((tm, d), lambda i, j: (i, 0)),
                  pl.BlockSpec((None, 1, d), seq),
                  pl.BlockSpec((None, 1, d), seq),
                  pl.BlockSpec((1, d), lambda i, j: (0, 0)),
                  pl.BlockSpec((tm, HEAD_DIM), lambda i, j: (i % spt, 0)),
                  pl.BlockSpec((tm, HEAD_DIM), lambda i, j: (i % spt, 0)),
                  pl.BlockSpec((d, gw), lambda i, j: (0, j))],
        out_specs=pl.BlockSpec((None, tm, gw), lambda i, j: (j, i, 0)),
        scratch_shapes=[pltpu.VMEM((tm, d), BF16)],
        compiler_params=_params(("parallel", "arbitrary")),
        name="inproj",
    )(x, shift, scale, norm_g, cos_t, sin_t, w_in)


def _t5_bucket(rel):
    half = N_BUCKETS // 2
    max_exact = half // 2
    ret = jnp.where(rel > 0, half, 0)
    n = jnp.abs(rel)
    nf = jnp.maximum(n, 1).astype(F32)
    large = max_exact + (jnp.log(nf / max_exact) / math.log(REL_MAX_DIST / max_exact)
                         * (half - max_exact)).astype(jnp.int32)
    large = jnp.minimum(large, half - 1)
    return ret + jnp.where(n < max_exact, n, large)


def _bias_kernel(rb_ref, bk_ref, o_ref):
    h = pl.program_id(0)
    for t in range(bk_ref.shape[0]):
        bk = bk_ref[t]
        acc = jnp.zeros(bk.shape, F32)
        for b in range(N_BUCKETS):
            acc = jnp.where(bk == b, rb_ref[b, h] * LOG2E, acc)
        o_ref[t] = acc


def _bias_tiles(rel_bias, t):
    nh = rel_bias.shape[1]
    i = jnp.arange(t, dtype=jnp.int32)
    rel0 = i[None, :] - i[:, None]
    buckets = jnp.stack([_t5_bucket(rel0 + d) for d in (-t, 0, t)])
    return pl.pallas_call(
        _bias_kernel,
        out_shape=jax.ShapeDtypeStruct((nh, 3, t, t), F32),
        grid=(nh,),
        in_specs=[pl.BlockSpec(memory_space=pltpu.SMEM),
                  pl.BlockSpec((3, t, t), lambda h: (0, 0, 0))],
        out_specs=pl.BlockSpec((None, 3, t, t), lambda h: (h, 0, 0, 0)),
        compiler_params=_params(("parallel",)),
        name="bias_tiles",
    )(rel_bias, buckets)


def _attn_kernel(rb_ref, lamp_ref, q_ref, k_ref, v_ref, bias_ref, g_ref, o_ref,
                 m_ref, l_ref, acc_ref, *, t, seq_len, lam_init):
    h = pl.program_id(1)
    qi = pl.program_id(2)
    nk = seq_len // t
    half = N_BUCKETS // 2

    q = q_ref[...]
    lane = lax.broadcasted_iota(jnp.int32, q.shape, 1)
    zero = jnp.zeros_like(q)
    qs = (jnp.where(lane < DIFF_QK_DIM, q, zero), jnp.where(lane >= DIFF_QK_DIM, q, zero))

    m_ref[...] = jnp.full(m_ref.shape, -jnp.inf, F32)
    l_ref[...] = jnp.zeros(l_ref.shape, F32)
    acc_ref[...] = jnp.zeros(acc_ref.shape, F32)

    def step(j, bias, const):
        off = pl.multiple_of(j * t, t)
        kt = k_ref[pl.ds(off, t), :]
        vt = v_ref[pl.ds(off, t), :]
        for idx in range(2):
            s = lax.dot_general(qs[idx], kt, (((1,), (1,)), ((), ())), preferred_element_type=F32)
            if bias is not None:
                s = s + bias
            m_old = m_ref[idx]
            m_t = jnp.max(s, axis=1, keepdims=True)
            if const is not None:
                m_new = jnp.maximum(m_old, m_t + const)
                shift = m_new - const
            else:
                m_new = jnp.maximum(m_old, m_t)
                shift = m_new
            alpha = jnp.exp2(m_old - m_new)
            p = jnp.exp2(s - shift)
            l_ref[idx] = alpha * l_ref[idx] + jnp.sum(p, axis=1, keepdims=True)
            m_ref[idx] = m_new
            acc_ref[idx] = alpha * acc_ref[idx] + jnp.dot(p.astype(BF16), vt, preferred_element_type=F32)

    c_left = rb_ref[half - 1, h] * LOG2E
    c_right = rb_ref[N_BUCKETS - 1, h] * LOG2E

    def far_left(j, carry):
        step(j, None, c_left)
        return carry

    def far_right(j, carry):
        step(j, None, c_right)
        return carry

    lax.fori_loop(0, jnp.maximum(qi - 1, 0), far_left, 0)
    for dj in (-1, 0, 1):
        @pl.when((qi + dj >= 0) & (qi + dj < nk))
        def _():
            step(qi + dj, bias_ref[dj + 1], None)
    lax.fori_loop(jnp.minimum(qi + 2, nk), nk, far_right, 0)

    lp = lamp_ref[...]
    lam = (jnp.exp(jnp.sum(lp[0:1] * lp[1:2], axis=1, keepdims=True))
           - jnp.exp(jnp.sum(lp[2:3] * lp[3:4], axis=1, keepdims=True)) + lam_init)
    out = acc_ref[0] / l_ref[0] - lam * (acc_ref[1] / l_ref[1])
    o_ref[...] = (_rms_rows(out, g_ref[...]) * (1.0 - lam_init)).astype(BF16)


def _diff_attention(proj, bias_tiles, rel_bias, lam_params, head_g, *, batch, seq_len, lam_init):
    n = proj.shape[1]
    nh = proj.shape[2] // HEAD_DIM
    t = ATTN_TILE
    nq = seq_len // t
    kv = lambda g: (lambda b, h, qi: (g, b, h))
    return pl.pallas_call(
        functools.partial(_attn_kernel, t=t, seq_len=seq_len, lam_init=lam_init),
        out_shape=jax.ShapeDtypeStruct((n, nh * HEAD_DIM), BF16),
        grid=(batch, nh, nq),
        in_specs=[pl.BlockSpec(memory_space=pltpu.SMEM),
                  pl.BlockSpec(lam_params.shape, lambda b, h, qi: (0, 0)),
                  pl.BlockSpec((None, t, HEAD_DIM), lambda b, h, qi: (0, b * nq + qi, h)),
                  pl.BlockSpec((None, seq_len, HEAD_DIM), kv(1)),
                  pl.BlockSpec((None, seq_len, HEAD_DIM), kv(2)),
                  pl.BlockSpec((None, 3, t, t), lambda b, h, qi: (h, 0, 0, 0)),
                  pl.BlockSpec((1, HEAD_DIM), lambda b, h, qi: (0, 0))],
        out_specs=pl.BlockSpec((t, HEAD_DIM), lambda b, h, qi: (b * nq + qi, h)),
        scratch_shapes=[pltpu.VMEM((2, t, 1), F32), pltpu.VMEM((2, t, 1), F32),
                        pltpu.VMEM((2, t, HEAD_DIM), F32)],
        compiler_params=_params(("parallel", "parallel", "arbitrary")),
        name="diff_attention",
    )(rel_bias, lam_params, proj, proj, proj, bias_tiles, head_g)


def _log_sigmoid(x):
    return jnp.minimum(x, 0.0) - jnp.log1p(jnp.exp(-jnp.abs(x)))


def _ret_kernel(df_ref, db_ref, q_ref, k_ref, v_ref, rg_ref, g_ref, o_ref, u_ref, r_ref, *, seq_len, c):
    h = pl.program_id(1)
    n = seq_len // c
    d = HEAD_DIM
    df, db = df_ref[0, h], db_ref[0, h]

    lgf = _log_sigmoid(jnp.full((c, d), df, F32))
    lgb = _log_sigmoid(jnp.full((c, d), db, F32))
    ri = lax.broadcasted_iota(jnp.int32, (c, d), 0).astype(F32)
    xi_f, ze_f = jnp.exp(lgf * (ri + 1.0)), jnp.exp(lgf * (c - 1.0 - ri))
    xi_b, ze_b = jnp.exp(lgb * (c - ri)), jnp.exp(lgb * ri)
    gc_f = jnp.exp(_log_sigmoid(jnp.full((d, d), df, F32)) * c)
    gc_b = jnp.exp(_log_sigmoid(jnp.full((d, d), db, F32)) * c)

    diff = (lax.broadcasted_iota(jnp.int32, (c, c), 0) - lax.broadcasted_iota(jnp.int32, (c, c), 1)).astype(F32)
    lgf2 = _log_sigmoid(jnp.full((c, c), df, F32))
    lgb2 = _log_sigmoid(jnp.full((c, c), db, F32))
    decay = jnp.where(diff >= 0, jnp.exp(lgf2 * jnp.maximum(diff, 0.0)), jnp.exp(lgb2 * jnp.maximum(-diff, 0.0)))

    def chunk(ref, i):
        return ref[pl.ds(pl.multiple_of(i * c, c), c), :]

    def local_state(i, carry):
        kc = chunk(k_ref, i).astype(F32)
        kz = jnp.concatenate([kc * ze_f, kc * ze_b], axis=1).astype(BF16)
        u_ref[i] = lax.dot_general(kz, chunk(v_ref, i), (((0,), (0,)), ((), ())), preferred_element_type=F32)
        return carry

    lax.fori_loop(0, n, local_state, 0)

    def scan_fwd(i, r):
        r_ref[i, :d, :] = r.astype(BF16)
        return gc_f * r + u_ref[i, :d, :]

    def scan_bwd(i, r):
        i = n - 1 - i
        r_ref[i, d:, :] = r.astype(BF16)
        return gc_b * r + u_ref[i, d:, :]

    lax.fori_loop(0, n, scan_fwd, jnp.zeros((d, d), F32))
    lax.fori_loop(0, n, scan_bwd, jnp.zeros((d, d), F32))

    def outputs(i, carry):
        qb = chunk(q_ref, i)
        qc = qb.astype(F32)
        vc = chunk(v_ref, i)
        qx = jnp.concatenate([qc * xi_f, qc * xi_b], axis=1).astype(BF16)
        inter = jnp.dot(qx, r_ref[i], preferred_element_type=F32)
        inner = lax.dot_general(qb, chunk(k_ref, i), (((1,), (1,)), ((), ())), preferred_element_type=F32) * decay
        o = inter + jnp.dot(inner.astype(BF16), vc, preferred_element_type=F32)
        rg = chunk(rg_ref, i).astype(F32)
        o_ref[pl.ds(pl.multiple_of(i * c, c), c), :] = (_rms_rows(o, g_ref[...]) * _silu(rg)).astype(BF16)
        return carry

    lax.fori_loop(0, n, outputs, 0)


def _retention(proj, decay_f, decay_b, head_g, *, batch, seq_len):
    n = proj.shape[1]
    nh = proj.shape[2] // HEAD_DIM
    c = RET_CHUNK
    blk = lambda g: pl.BlockSpec((None, seq_len, HEAD_DIM), lambda b, h: (g, b, h))
    smem = pl.BlockSpec(memory_space=pltpu.SMEM)
    return pl.pallas_call(
        functools.partial(_ret_kernel, seq_len=seq_len, c=c),
        out_shape=jax.ShapeDtypeStruct((n, nh * HEAD_DIM), BF16),
        grid=(batch, nh),
        in_specs=[smem, smem, blk(3), blk(4), blk(5), blk(6),
                  pl.BlockSpec((1, HEAD_DIM), lambda b, h: (0, 0))],
        out_specs=pl.BlockSpec((seq_len, HEAD_DIM), lambda b, h: (b, h)),
        scratch_shapes=[pltpu.VMEM((seq_len // c, 2 * HEAD_DIM, HEAD_DIM), F32),
                        pltpu.VMEM((seq_len // c, 2 * HEAD_DIM, HEAD_DIM), BF16)],
        compiler_params=_params(("parallel", "parallel")),
        name="retention",
    )(decay_f, decay_b, proj, proj, proj, proj, head_g)


def _outproj_kernel(x_ref, gt_ref, a_ref, r_ref, wa_ref, wr_ref, o_ref):
    y = (jnp.dot(a_ref[...], wa_ref[...], preferred_element_type=F32)
         + jnp.dot(r_ref[...], wr_ref[...], preferred_element_type=F32))
    o_ref[...] = x_ref[...] + gt_ref[...] * y


def _outproj(x, gate, d_out, r_out, w_out, *, seq_len):
    n, d = x.shape
    kw = d_out.shape[1]
    tm = PROJ_ROW_TILE
    row = lambda i: (i, 0)
    return pl.pallas_call(
        _outproj_kernel,
        out_shape=jax.ShapeDtypeStruct((n, d), F32),
        grid=(n // tm,),
        in_specs=[pl.BlockSpec((tm, d), row),
                  pl.BlockSpec((None, 1, d), lambda i: ((i * tm) // seq_len, 0, 0)),
                  pl.BlockSpec((tm, kw), row),
                  pl.BlockSpec((tm, kw), row),
                  pl.BlockSpec((kw, d), lambda i: (0, 0)),
                  pl.BlockSpec((kw, d), lambda i: (1, 0))],
        out_specs=pl.BlockSpec((tm, d), row),
        compiler_params=_params(("parallel",)),
        name="outproj",
    )(x, gate, d_out, r_out, w_out, w_out)


def _rotary_tables(seq_len):
    pos = jnp.arange(seq_len, dtype=F32)
    inv = ROPE_BASE ** (-jnp.arange(0, HEAD_DIM, 2, dtype=F32) / HEAD_DIM)
    ang = pos[:, None] * inv[None, :]
    cos, sin = jnp.cos(ang), jnp.sin(ang)
    return jnp.concatenate([cos, cos], axis=1), jnp.concatenate([-sin, sin], axis=1)


def _trunk(x, mod, w, bias_tiles):
    batch, seq_len, d = x.shape
    x = x.reshape(batch * seq_len, d)
    sh1, sc1, g1, shm, scm, gm, sh2, sc2, g2 = [mod[:, i].reshape(batch, 1, d) for i in range(N_MOD)]
    lam_init = 0.8 - 0.6 * math.exp(-0.3 * 0)
    cos_t, sin_t = _rotary_tables(seq_len)

    x = _ffn(x, sh1, sc1, g1, w["ffn1_norm_g"], w["final_norm_g"], w["ffn1_w13"], w["ffn1_w2"],
             seq_len=seq_len, final_norm=False)
    proj = _inproj(x, shm, scm, w["mix_norm_g"], cos_t, sin_t, w["w_in"], seq_len=seq_len)
    d_out = _diff_attention(proj, bias_tiles, w["rel_bias"], w["lam_params"], w["diff_head_g"],
                            batch=batch, seq_len=seq_len, lam_init=lam_init)
    r_out = _retention(proj, w["ret_decay_fwd"], w["ret_decay_bwd"], w["ret_head_g"],
                       batch=batch, seq_len=seq_len)
    x = _outproj(x, gm, d_out, r_out, w["w_out"], seq_len=seq_len)
    x = _ffn(x, sh2, sc2, g2, w["ffn2_norm_g"], w["final_norm_g"], w["ffn2_w13"], w["ffn2_w2"],
             seq_len=seq_len, final_norm=True)
    return x.reshape(batch, seq_len, d)


def kernel(x_prompt, x_sample, c_prompt, c_sample, ada_w, ada_b, ffn1_norm_g, ffn1_w13, ffn1_w2, mix_norm_g, w_in, diff_lambda_q1, diff_lambda_k1, diff_lambda_q2, diff_lambda_k2, diff_head_g, rel_bias, ret_decay_fwd, ret_decay_bwd, ret_head_g, w_out, ffn2_norm_g, ffn2_w13, ffn2_w2, final_norm_g):
    assert ada_w.shape[0] == 1, "single-layer trunk"
    d = x_prompt.shape[-1]
    nb = c_prompt.shape[0]
    row = lambda a: a.reshape(1, -1)
    w = {
        "ffn1_norm_g": ffn1_norm_g, "mix_norm_g": mix_norm_g, "ffn2_norm_g": ffn2_norm_g,
        "final_norm_g": row(final_norm_g), "diff_head_g": diff_head_g, "ret_head_g": ret_head_g,
        "ffn1_w13": ffn1_w13[0].astype(BF16), "ffn1_w2": ffn1_w2[0].astype(BF16),
        "ffn2_w13": ffn2_w13[0].astype(BF16), "ffn2_w2": ffn2_w2[0].astype(BF16),
        "w_in": w_in[0].astype(BF16), "w_out": w_out[0].astype(BF16),
        "rel_bias": rel_bias, "ret_decay_fwd": ret_decay_fwd, "ret_decay_bwd": ret_decay_bwd,
        "lam_params": jnp.concatenate([diff_lambda_q1, diff_lambda_k1, diff_lambda_q2, diff_lambda_k2], axis=0),
    }
    mod = _ada_mod(jnp.concatenate([c_prompt, c_sample], axis=0), ada_w[0], ada_b[0])
    mod = mod.reshape(mod.shape[0], N_MOD, d)
    bias_tiles = _bias_tiles(rel_bias, ATTN_TILE)
    y_prompt = _trunk(x_prompt, mod[:nb], w, bias_tiles)
    y_sample = _trunk(x_sample, mod[nb:], w, bias_tiles)
    return (y_prompt, y_sample)
```

```python
import functools
import math

import jax
import jax.numpy as jnp
from jax import lax
from jax.experimental import pallas as pl
from jax.experimental.pallas import tpu as pltpu

F32 = jnp.float32
BF16 = jnp.bfloat16

HEAD_DIM = 128
DIFF_QK_DIM = HEAD_DIM // 2
N_BUCKETS = 32
REL_MAX_DIST = 128
ROPE_BASE = 10000.0
N_MOD = 9
EPS = 1e-6
LOG2E = math.log2(math.e)
N_PROJ = 7

V7X_VMEM_BYTES = 64 * 1024 * 1024
VMEM_LIMIT = 56 * 1024 * 1024

FFN_ROW_TILE = 512
FFN_HID_TILE = 512
PROJ_ROW_TILE = 512
ATTN_TILE = 512
RET_CHUNK = 256
ADA_COL_TILE = 1024


def _params(sem):
    return pltpu.CompilerParams(dimension_semantics=sem, vmem_limit_bytes=VMEM_LIMIT)


def _rms_rows(x, g):
    return x * lax.rsqrt(jnp.mean(x * x, axis=-1, keepdims=True) + EPS) * g


def _silu(x):
    return x * jax.nn.sigmoid(x)


def _ada_kernel(c_ref, w_ref, b_ref, o_ref):
    a = _silu(c_ref[...]).astype(BF16)
    o_ref[...] = jnp.dot(a, w_ref[...].astype(BF16), preferred_element_type=F32) + b_ref[...]


def _ada_mod(c, w, b):
    nb, d = c.shape
    n = w.shape[1]
    tn = ADA_COL_TILE
    return pl.pallas_call(
        _ada_kernel,
        out_shape=jax.ShapeDtypeStruct((nb, n), F32),
        grid=(n // tn,),
        in_specs=[pl.BlockSpec((nb, d), lambda j: (0, 0)),
                  pl.BlockSpec((d, tn), lambda j: (0, j)),
                  pl.BlockSpec((1, tn), lambda j: (0, j))],
        out_specs=pl.BlockSpec((nb, tn), lambda j: (0, j)),
        compiler_params=_params(("parallel",)),
        name="ada_mod",
    )(c, w, b.reshape(1, n))


def _ffn_kernel(x_ref, sh_ref, sc_ref, gt_ref, ng_ref, fg_ref, w1_ref, w3_ref, w2_ref,
                o_ref, h_ref, *, final_norm):
    j = pl.program_id(1)

    @pl.when(j == 0)
    def _():
        h = _rms_rows(x_ref[...], ng_ref[...]) * (1.0 + sc_ref[...]) + sh_ref[...]
        h_ref[...] = h.astype(BF16)

    h = h_ref[...]
    g = jnp.dot(h, w1_ref[...], preferred_element_type=F32)
    u = jnp.dot(h, w3_ref[...], preferred_element_type=F32)
    part = jnp.dot((_silu(g) * u).astype(BF16), w2_ref[...], preferred_element_type=F32)

    @pl.when(j == 0)
    def _():
        o_ref[...] = part

    @pl.when(j > 0)
    def _():
        o_ref[...] += part

    @pl.when(j == pl.num_programs(1) - 1)
    def _():
        y = x_ref[...] + 0.5 * gt_ref[...] * o_ref[...]
        if final_norm:
            y = _rms_rows(y, fg_ref[...])
        o_ref[...] = y


def _ffn(x, shift, scale, gate, norm_g, final_g, w13, w2, *, seq_len, final_norm):
    n, d = x.shape
    f = w2.shape[0]
    tm, tf = FFN_ROW_TILE, FFN_HID_TILE
    nf = f // tf
    seq = lambda i, j: ((i * tm) // seq_len, 0, 0)
    row = lambda i, j: (i, 0)
    const = lambda i, j: (0, 0)
    return pl.pallas_call(
        functools.partial(_ffn_kernel, final_norm=final_norm),
        out_shape=jax.ShapeDtypeStruct((n, d), F32),
        grid=(n // tm, nf),
        in_specs=[pl.BlockSpec((tm, d), row),
                  pl.BlockSpec((None, 1, d), seq),
                  pl.BlockSpec((None, 1, d), seq),
                  pl.BlockSpec((None, 1, d), seq),
                  pl.BlockSpec((1, d), const),
                  pl.BlockSpec((1, d), const),
                  pl.BlockSpec((d, tf), lambda i, j: (0, j)),
                  pl.BlockSpec((d, tf), lambda i, j: (0, nf + j)),
                  pl.BlockSpec((tf, d), lambda i, j: (j, 0))],
        out_specs=pl.BlockSpec((tm, d), row),
        scratch_shapes=[pltpu.VMEM((tm, d), BF16)],
        compiler_params=_params(("parallel", "arbitrary")),
        name="ffn_final" if final_norm else "ffn",
    )(x, shift, scale, gate, norm_g, final_g, w13, w13, w2)


def _inproj_kernel(x_ref, sh_ref, sc_ref, ng_ref, cq_ref, sq_ref, w_ref, o_ref, h_ref):
    j = pl.program_id(1)

    @pl.when(j == 0)
    def _():
        h = _rms_rows(x_ref[...], ng_ref[...]) * (1.0 + sc_ref[...]) + sh_ref[...]
        h_ref[...] = h.astype(BF16)

    res = jnp.dot(h_ref[...], w_ref[...], preferred_element_type=F32)

    def rotary(scale):
        cos, sin = cq_ref[...], sq_ref[...]
        for hd in range(res.shape[1] // HEAD_DIM):
            xs = res[:, hd * HEAD_DIM:(hd + 1) * HEAD_DIM]
            rot = pltpu.roll(xs, HEAD_DIM // 2, 1)
            o_ref[:, hd * HEAD_DIM:(hd + 1) * HEAD_DIM] = ((xs * cos + rot * sin) * scale).astype(BF16)

    @pl.when(j == 0)
    def _():
        o_ref[...] = (res * (DIFF_QK_DIM ** -0.5 * LOG2E)).astype(BF16)

    @pl.when(j == 3)
    def _():
        rotary(1.0)

    @pl.when(j == 4)
    def _():
        rotary(HEAD_DIM ** -0.5)

    @pl.when((j != 0) & (j != 3) & (j != 4))
    def _():
        o_ref[...] = res.astype(BF16)


def _inproj(x, shift, scale, norm_g, cos_t, sin_t, w_in, *, seq_len):
    n, d = x.shape
    gw = w_in.shape[1] // N_PROJ
    tm = PROJ_ROW_TILE
    spt = seq_len // tm
    seq = lambda i, j: ((i * tm) // seq_len, 0, 0)
    return pl.pallas_call(
        _inproj_kernel,
        out_shape=jax.ShapeDtypeStruct((N_PROJ, n, gw), BF16),
        grid=(n // tm, N_PROJ),
        in_specs=[pl.BlockSpec((tm, d), lambda i, j: (i, 0)),
                  pl.BlockSpec((None, 1, d), seq),
                  pl.BlockSpec((None, 1, d), seq),
                  pl.BlockSpec((1, d), lambda i, j: (0, 0)),
                  pl.BlockSpec((tm, HEAD_DIM), lambda i, j: (i % spt, 0)),
                  pl.BlockSpec((tm, HEAD_DIM), lambda i, j: (i % spt, 0)),
                  pl.BlockSpec((d, gw), lambda i, j: (0, j))],
        out_specs=pl.BlockSpec((None, tm, gw), lambda i, j: (j, i, 0)),
        scratch_shapes=[pltpu.VMEM((tm, d), BF16)],
        compiler_params=_params(("parallel", "arbitrary")),
        name="inproj",
    )(x, shift, scale, norm_g, cos_t, sin_t, w_in)


def _t5_bucket(rel):
    half = N_BUCKETS // 2
    max_exact = half // 2
    ret = jnp.where(rel > 0, half, 0)
    n = jnp.abs(rel)
    nf = jnp.maximum(n, 1).astype(F32)
    large = max_exact + (jnp.log(nf / max_exact) / math.log(REL_MAX_DIST / max_exact)
                         * (half - max_exact)).astype(jnp.int32)
    large = jnp.minimum(large, half - 1)
    return ret + jnp.where(n < max_exact, n, large)


def _bias_kernel(rb_ref, bk_ref, o_ref):
    h = pl.program_id(0)
    for t in range(bk_ref.shape[0]):
        bk = bk_ref[t]
        acc = jnp.zeros(bk.shape, F32)
        for b in range(N_BUCKETS):
            acc = jnp.where(bk == b, rb_ref[b, h] * LOG2E, acc)
        o_ref[t] = acc


def _bias_tiles(rel_bias, t):
    nh = rel_bias.shape[1]
    i = jnp.arange(t, dtype=jnp.int32)
    rel0 = i[:, None] - i[None, :]
    buckets = jnp.stack([_t5_bucket(rel0 + d) for d in (-t, 0, t)])
    return pl.pallas_call(
        _bias_kernel,
        out_shape=jax.ShapeDtypeStruct((nh, 3, t, t), F32),
        grid=(nh,),
        in_specs=[pl.BlockSpec(memory_space=pltpu.SMEM),
                  pl.BlockSpec((3, t, t), lambda h: (0, 0, 0))],
        out_specs=pl.BlockSpec((None, 3, t, t), lambda h: (h, 0, 0, 0)),
        compiler_params=_params(("parallel",)),
        name="bias_tiles",
    )(rel_bias, buckets)


def _attn_kernel(rb_ref, lamp_ref, q_ref, k_ref, v_ref, bias_ref, g_ref, o_ref,
                 vt_ref, m_ref, l_ref, acc_ref, *, t, seq_len, lam_init):
    h = pl.program_id(1)
    qi = pl.program_id(2)
    nk = seq_len // t
    half = N_BUCKETS // 2

    @pl.when(qi == 0)
    def _():
        for c in range(nk):
            vt_ref[:, c * t:(c + 1) * t] = v_ref[c * t:(c + 1) * t, :].astype(F32).T.astype(BF16)

    qt = q_ref[...].astype(F32).T
    row = lax.broadcasted_iota(jnp.int32, qt.shape, 0)
    qts = (jnp.where(row < DIFF_QK_DIM, qt, 0.0).astype(BF16), jnp.where(row >= DIFF_QK_DIM, qt, 0.0).astype(BF16))

    m_ref[...] = jnp.full(m_ref.shape, -jnp.inf, F32)
    l_ref[...] = jnp.zeros(l_ref.shape, F32)
    acc_ref[...] = jnp.zeros(acc_ref.shape, F32)

    def step(j, bias, const):
        off = pl.multiple_of(j * t, t)
        kt = k_ref[pl.ds(off, t), :]
        vt = vt_ref[:, pl.ds(off, t)]
        for idx in range(2):
            s = jnp.dot(kt, qts[idx], preferred_element_type=F32)
            if bias is not None:
                s = s + bias
            m_old = m_ref[idx]
            m_t = jnp.max(s, axis=0, keepdims=True)
            if const is not None:
                m_new = jnp.maximum(m_old, m_t + const)
                shift = m_new - const
            else:
                m_new = jnp.maximum(m_old, m_t)
                shift = m_new
            alpha = jnp.exp2(m_old - m_new)
            p = jnp.exp2(s - shift)
            l_ref[idx] = alpha * l_ref[idx] + jnp.sum(p, axis=0, keepdims=True)
            m_ref[idx] = m_new
            acc_ref[idx] = alpha * acc_ref[idx] + jnp.dot(vt, p.astype(BF16), preferred_element_type=F32)

    c_left = rb_ref[half - 1, h] * LOG2E
    c_right = rb_ref[N_BUCKETS - 1, h] * LOG2E

    def far_left(j, carry):
        step(j, None, c_left)
        return carry

    def far_right(j, carry):
        step(j, None, c_right)
        return carry

    lax.fori_loop(0, jnp.maximum(qi - 1, 0), far_left, 0)
    for dj in (-1, 0, 1):
        @pl.when((qi + dj >= 0) & (qi + dj < nk))
        def _():
            step(qi + dj, bias_ref[dj + 1], None)
    lax.fori_loop(jnp.minimum(qi + 2, nk), nk, far_right, 0)

    lp = lamp_ref[...]
    lam = (jnp.exp(jnp.sum(lp[0:1] * lp[1:2], axis=1, keepdims=True))
           - jnp.exp(jnp.sum(lp[2:3] * lp[3:4], axis=1, keepdims=True)) + lam_init)
    out = acc_ref[0] / l_ref[0] - lam * (acc_ref[1] / l_ref[1])
    y = out * lax.rsqrt(jnp.mean(out * out, axis=0, keepdims=True) + EPS)
    o_ref[...] = (y.T * g_ref[...] * (1.0 - lam_init)).astype(BF16)


def _diff_attention(proj, bias_tiles, rel_bias, lam_params, head_g, *, batch, seq_len, lam_init):
    n = proj.shape[1]
    nh = proj.shape[2] // HEAD_DIM
    t = ATTN_TILE
    nq = seq_len // t
    kv = lambda g: (lambda b, h, qi: (g, b, h))
    return pl.pallas_call(
        functools.partial(_attn_kernel, t=t, seq_len=seq_len, lam_init=lam_init),
        out_shape=jax.ShapeDtypeStruct((n, nh * HEAD_DIM), BF16),
        grid=(batch, nh, nq),
        in_specs=[pl.BlockSpec(memory_space=pltpu.SMEM),
                  pl.BlockSpec(lam_params.shape, lambda b, h, qi: (0, 0)),
                  pl.BlockSpec((None, t, HEAD_DIM), lambda b, h, qi: (0, b * nq + qi, h)),
                  pl.BlockSpec((None, seq_len, HEAD_DIM), kv(1)),
                  pl.BlockSpec((None, seq_len, HEAD_DIM), kv(2)),
                  pl.BlockSpec((None, 3, t, t), lambda b, h, qi: (h, 0, 0, 0)),
                  pl.BlockSpec((1, HEAD_DIM), lambda b, h, qi: (0, 0))],
        out_specs=pl.BlockSpec((t, HEAD_DIM), lambda b, h, qi: (b * nq + qi, h)),
        scratch_shapes=[pltpu.VMEM((HEAD_DIM, seq_len), BF16),
                        pltpu.VMEM((2, 1, t), F32), pltpu.VMEM((2, 1, t), F32),
                        pltpu.VMEM((2, HEAD_DIM, t), F32)],
        compiler_params=_params(("parallel", "parallel", "arbitrary")),
        name="diff_attention",
    )(rel_bias, lam_params, proj, proj, proj, bias_tiles, head_g)


def _log_sigmoid(x):
    return jnp.minimum(x, 0.0) - jnp.log1p(jnp.exp(-jnp.abs(x)))


def _ret_kernel(df_ref, db_ref, q_ref, k_ref, v_ref, rg_ref, g_ref, o_ref, u_ref, r_ref, *, seq_len, c):
    h = pl.program_id(1)
    n = seq_len // c
    d = HEAD_DIM
    df, db = df_ref[0, h], db_ref[0, h]

    lgf = _log_sigmoid(jnp.full((c, d), df, F32))
    lgb = _log_sigmoid(jnp.full((c, d), db, F32))
    ri = lax.broadcasted_iota(jnp.int32, (c, d), 0).astype(F32)
    xi_f, ze_f = jnp.exp(lgf * (ri + 1.0)), jnp.exp(lgf * (c - 1.0 - ri))
    xi_b, ze_b = jnp.exp(lgb * (c - ri)), jnp.exp(lgb * ri)
    gc_f = jnp.exp(_log_sigmoid(jnp.full((d, d), df, F32)) * c)
    gc_b = jnp.exp(_log_sigmoid(jnp.full((d, d), db, F32)) * c)

    diff = (lax.broadcasted_iota(jnp.int32, (c, c), 0) - lax.broadcasted_iota(jnp.int32, (c, c), 1)).astype(F32)
    lgf2 = _log_sigmoid(jnp.full((c, c), df, F32))
    lgb2 = _log_sigmoid(jnp.full((c, c), db, F32))
    decay = jnp.where(diff >= 0, jnp.exp(lgf2 * jnp.maximum(diff, 0.0)), jnp.exp(lgb2 * jnp.maximum(-diff, 0.0)))

    def chunk(ref, i):
        return ref[pl.ds(pl.multiple_of(i * c, c), c), :]

    def local_state(i, carry):
        kc = chunk(k_ref, i).astype(F32)
        kz = jnp.concatenate([kc * ze_f, kc * ze_b], axis=1).astype(BF16)
        u_ref[i] = lax.dot_general(kz, chunk(v_ref, i), (((0,), (0,)), ((), ())), preferred_element_type=F32)
        return carry

    lax.fori_loop(0, n, local_state, 0)

    def scan_fwd(i, r):
        r_ref[i, :d, :] = r.astype(BF16)
        return gc_f * r + u_ref[i, :d, :]

    def scan_bwd(i, r):
        i = n - 1 - i
        r_ref[i, d:, :] = r.astype(BF16)
        return gc_b * r + u_ref[i, d:, :]

    lax.fori_loop(0, n, scan_fwd, jnp.zeros((d, d), F32))
    lax.fori_loop(0, n, scan_bwd, jnp.zeros((d, d), F32))

    def outputs(i, carry):
        qb = chunk(q_ref, i)
        qc = qb.astype(F32)
        vc = chunk(v_ref, i)
        qx = jnp.concatenate([qc * xi_f, qc * xi_b], axis=1).astype(BF16)
        inter = jnp.dot(qx, r_ref[i], preferred_element_type=F32)
        inner = lax.dot_general(qb, chunk(k_ref, i), (((1,), (1,)), ((), ())), preferred_element_type=F32) * decay
        o = inter + jnp.dot(inner.astype(BF16), vc, preferred_element_type=F32)
        rg = chunk(rg_ref, i).astype(F32)
        o_ref[pl.ds(pl.multiple_of(i * c, c), c), :] = (_rms_rows(o, g_ref[...]) * _silu(rg)).astype(BF16)
        return carry

    lax.fori_loop(0, n, outputs, 0)


def _retention(proj, decay_f, decay_b, head_g, *, batch, seq_len):
    n = proj.shape[1]
    nh = proj.shape[2] // HEAD_DIM
    c = RET_CHUNK
    blk = lambda g: pl.BlockSpec((None, seq_len, HEAD_DIM), lambda b, h: (g, b, h))
    smem = pl.BlockSpec(memory_space=pltpu.SMEM)
    return pl.pallas_call(
        functools.partial(_ret_kernel, seq_len=seq_len, c=c),
        out_shape=jax.ShapeDtypeStruct((n, nh * HEAD_DIM), BF16),
        grid=(batch, nh),
        in_specs=[smem, smem, blk(3), blk(4), blk(5), blk(6),
                  pl.BlockSpec((1, HEAD_DIM), lambda b, h: (0, 0))],
        out_specs=pl.BlockSpec((seq_len, HEAD_DIM), lambda b, h: (b, h)),
        scratch_shapes=[pltpu.VMEM((seq_len // c, 2 * HEAD_DIM, HEAD_DIM), F32),
                        pltpu.VMEM((seq_len // c, 2 * HEAD_DIM, HEAD_DIM), BF16)],
        compiler_params=_params(("parallel", "parallel")),
        name="retention",
    )(decay_f, decay_b, proj, proj, proj, proj, head_g)


def _outproj_kernel(x_ref, gt_ref, a_ref, r_ref, wa_ref, wr_ref, o_ref):
    y = (jnp.dot(a_ref[...], wa_ref[...], preferred_element_type=F32)
         + jnp.dot(r_ref[...], wr_ref[...], preferred_element_type=F32))
    o_ref[...] = x_ref[...] + gt_ref[...] * y


def _outproj(x, gate, d_out, r_out, w_out, *, seq_len):
    n, d = x.shape
    kw = d_out.shape[1]
    tm = PROJ_ROW_TILE
    row = lambda i: (i, 0)
    return pl.pallas_call(
        _outproj_kernel,
        out_shape=jax.ShapeDtypeStruct((n, d), F32),
        grid=(n // tm,),
        in_specs=[pl.BlockSpec((tm, d), row),
                  pl.BlockSpec((None, 1, d), lambda i: ((i * tm) // seq_len, 0, 0)),
                  pl.BlockSpec((tm, kw), row),
                  pl.BlockSpec((tm, kw), row),
                  pl.BlockSpec((kw, d), lambda i: (0, 0)),
                  pl.BlockSpec((kw, d), lambda i: (1, 0))],
        out_specs=pl.BlockSpec((tm, d), row),
        compiler_params=_params(("parallel",)),
        name="outproj",
    )(x, gate, d_out, r_out, w_out, w_out)


def _rotary_tables(seq_len):
    pos = jnp.arange(seq_len, dtype=F32)
    inv = ROPE_BASE ** (-jnp.arange(0, HEAD_DIM, 2, dtype=F32) / HEAD_DIM)
    ang = pos[:, None] * inv[None, :]
    cos, sin = jnp.cos(ang), jnp.sin(ang)
    return jnp.concatenate([cos, cos], axis=1), jnp.concatenate([-sin, sin], axis=1)


def _trunk(x, mod, w, bias_tiles):
    batch, seq_len, d = x.shape
    x = x.reshape(batch * seq_len, d)
    sh1, sc1, g1, shm, scm, gm, sh2, sc2, g2 = [mod[:, i].reshape(batch, 1, d) for i in range(N_MOD)]
    lam_init = 0.8 - 0.6 * math.exp(-0.3 * 0)
    cos_t, sin_t = _rotary_tables(seq_len)

    x = _ffn(x, sh1, sc1, g1, w["ffn1_norm_g"], w["final_norm_g"], w["ffn1_w13"], w["ffn1_w2"],
             seq_len=seq_len, final_norm=False)
    proj = _inproj(x, shm, scm, w["mix_norm_g"], cos_t, sin_t, w["w_in"], seq_len=seq_len)
    d_out = _diff_attention(proj, bias_tiles, w["rel_bias"], w["lam_params"], w["diff_head_g"],
                            batch=batch, seq_len=seq_len, lam_init=lam_init)
    r_out = _retention(proj, w["ret_decay_fwd"], w["ret_decay_bwd"], w["ret_head_g"],
                       batch=batch, seq_len=seq_len)
    x = _outproj(x, gm, d_out, r_out, w["w_out"], seq_len=seq_len)
    x = _ffn(x, sh2, sc2, g2, w["ffn2_norm_g"], w["final_norm_g"], w["ffn2_w13"], w["ffn2_w2"],
             seq_len=seq_len, final_norm=True)
    return x.reshape(batch, seq_len, d)


def kernel(x_prompt, x_sample, c_prompt, c_sample, ada_w, ada_b, ffn1_norm_g, ffn1_w13, ffn1_w2, mix_norm_g, w_in, diff_lambda_q1, diff_lambda_k1, diff_lambda_q2, diff_lambda_k2, diff_head_g, rel_bias, ret_decay_fwd, ret_decay_bwd, ret_head_g, w_out, ffn2_norm_g, ffn2_w13, ffn2_w2, final_norm_g):
    assert ada_w.shape[0] == 1, "single-layer trunk"
    d = x_prompt.shape[-1]
    nb = c_prompt.shape[0]
    row = lambda a: a.reshape(1, -1)
    w = {
        "ffn1_norm_g": ffn1_norm_g, "mix_norm_g": mix_norm_g, "ffn2_norm_g": ffn2_norm_g,
        "final_norm_g": row(final_norm_g), "diff_head_g": diff_head_g, "ret_head_g": ret_head_g,
        "ffn1_w13": ffn1_w13[0].astype(BF16), "ffn1_w2": ffn1_w2[0].astype(BF16),
        "ffn2_w13": ffn2_w13[0].astype(BF16), "ffn2_w2": ffn2_w2[0].astype(BF16),
        "w_in": w_in[0].astype(BF16), "w_out": w_out[0].astype(BF16),
        "rel_bias": rel_bias, "ret_decay_fwd": ret_decay_fwd, "ret_decay_bwd": ret_decay_bwd,
        "lam_params": jnp.concatenate([diff_lambda_q1, diff_lambda_k1, diff_lambda_q2, diff_lambda_k2], axis=0),
    }
    mod = _ada_mod(jnp.concatenate([c_prompt, c_sample], axis=0), ada_w[0], ada_b[0])
    mod = mod.reshape(mod.shape[0], N_MOD, d)
    bias_tiles = _bias_tiles(rel_bias, ATTN_TILE)
    y_prompt = _trunk(x_prompt, mod[:nb], w, bias_tiles)
    y_sample = _trunk(x_sample, mod[nb:], w, bias_tiles)
    return (y_prompt, y_sample)
```

```python
import functools
import math

import jax
import jax.numpy as jnp
from jax import lax
from jax.experimental import pallas as pl
from jax.experimental.pallas import tpu as pltpu

F32 = jnp.float32
BF16 = jnp.bfloat16

HEAD_DIM = 128
DIFF_QK_DIM = HEAD_DIM // 2
N_BUCKETS = 32
REL_MAX_DIST = 128
ROPE_BASE = 10000.0
N_MOD = 9
EPS = 1e-6
LOG2E = math.log2(math.e)
N_PROJ = 7

V7X_VMEM_BYTES = 64 * 1024 * 1024
VMEM_LIMIT = 56 * 1024 * 1024

FFN_ROW_TILE = 512
FFN_HID_TILE = 512
PROJ_ROW_TILE = 512
NORM_ROW_BLOCK = 64
ATTN_TILE = 512
SUM_ROWS = 16
BIAS_REACH = 2
RET_CHUNK = 256
ADA_COL_TILE = 1024


def _params(sem):
    return pltpu.CompilerParams(dimension_semantics=sem, vmem_limit_bytes=VMEM_LIMIT)


def _rms_rows(x, g):
    return x * lax.rsqrt(jnp.mean(x * x, axis=-1, keepdims=True) + EPS) * g


def _silu(x):
    return x * jax.nn.sigmoid(x)


def _norm_modulate(x_ref, ng_ref, sc_ref, sh_ref, h_ref):
    gain = ng_ref[...] * (1.0 + sc_ref[...])
    shift = sh_ref[...]

    def block(i, carry):
        rows = pl.ds(pl.multiple_of(i * NORM_ROW_BLOCK, NORM_ROW_BLOCK), NORM_ROW_BLOCK)
        x = x_ref[rows, :]
        r = lax.rsqrt(jnp.mean(x * x, axis=-1, keepdims=True) + EPS)
        h_ref[rows, :] = (x * r * gain + shift).astype(BF16)
        return carry

    lax.fori_loop(0, x_ref.shape[0] // NORM_ROW_BLOCK, block, 0)


def _ada_kernel(c_ref, w_ref, b_ref, o_ref):
    a = _silu(c_ref[...]).astype(BF16)
    o_ref[...] = jnp.dot(a, w_ref[...].astype(BF16), preferred_element_type=F32) + b_ref[...]


def _ada_mod(c, w, b):
    nb, d = c.shape
    n = w.shape[1]
    tn = ADA_COL_TILE
    return pl.pallas_call(
        _ada_kernel,
        out_shape=jax.ShapeDtypeStruct((nb, n), F32),
        grid=(n // tn,),
        in_specs=[pl.BlockSpec((nb, d), lambda j: (0, 0)),
                  pl.BlockSpec((d, tn), lambda j: (0, j)),
                  pl.BlockSpec((1, tn), lambda j: (0, j))],
        out_specs=pl.BlockSpec((nb, tn), lambda j: (0, j)),
        compiler_params=_params(("parallel",)),
        name="ada_mod",
    )(c, w, b.reshape(1, n))


def _ffn_kernel(x_ref, sh_ref, sc_ref, gt_ref, ng_ref, fg_ref, w1_ref, w3_ref, w2_ref,
                o_ref, h_ref, *, final_norm):
    j = pl.program_id(1)

    @pl.when(j == 0)
    def _():
        _norm_modulate(x_ref, ng_ref, sc_ref, sh_ref, h_ref)
        o_ref[...] = jnp.zeros(o_ref.shape, F32)

    h = h_ref[...]
    g = jnp.dot(h, w1_ref[...], preferred_element_type=F32)
    u = jnp.dot(h, w3_ref[...], preferred_element_type=F32)
    o_ref[...] += jnp.dot((_silu(g) * u).astype(BF16), w2_ref[...], preferred_element_type=F32)

    @pl.when(j == pl.num_programs(1) - 1)
    def _():
        y = x_ref[...] + 0.5 * gt_ref[...] * o_ref[...]
        if final_norm:
            y = _rms_rows(y, fg_ref[...])
        o_ref[...] = y


def _ffn(x, shift, scale, gate, norm_g, final_g, w13, w2, *, seq_len, final_norm):
    n, d = x.shape
    f = w2.shape[0]
    tm, tf = FFN_ROW_TILE, FFN_HID_TILE
    nf = f // tf
    seq = lambda i, j: ((i * tm) // seq_len, 0, 0)
    row = lambda i, j: (i, 0)
    const = lambda i, j: (0, 0)
    return pl.pallas_call(
        functools.partial(_ffn_kernel, final_norm=final_norm),
        out_shape=jax.ShapeDtypeStruct((n, d), F32),
        grid=(n // tm, nf),
        in_specs=[pl.BlockSpec((tm, d), row),
                  pl.BlockSpec((None, 1, d), seq),
                  pl.BlockSpec((None, 1, d), seq),
                  pl.BlockSpec((None, 1, d), seq),
                  pl.BlockSpec((1, d), const),
                  pl.BlockSpec((1, d), const),
                  pl.BlockSpec((d, tf), lambda i, j: (0, j)),
                  pl.BlockSpec((d, tf), lambda i, j: (0, nf + j)),
                  pl.BlockSpec((tf, d), lambda i, j: (j, 0))],
        out_specs=pl.BlockSpec((tm, d), row),
        scratch_shapes=[pltpu.VMEM((tm, d), BF16)],
        compiler_params=_params(("parallel", "arbitrary")),
        name="ffn_final" if final_norm else "ffn",
    )(x, shift, scale, gate, norm_g, final_g, w13, w13, w2)


def _inproj_kernel(x_ref, sh_ref, sc_ref, ng_ref, cq_ref, sq_ref, w_ref, o_ref, h_ref):
    j = pl.program_id(1)

    @pl.when(j == 0)
    def _():
        _norm_modulate(x_ref, ng_ref, sc_ref, sh_ref, h_ref)

    res = jnp.dot(h_ref[...], w_ref[...], preferred_element_type=F32)

    def rotary(scale):
        cos, sin = cq_ref[...], sq_ref[...]
        for hd in range(res.shape[1] // HEAD_DIM):
            xs = res[:, hd * HEAD_DIM:(hd + 1) * HEAD_DIM]
            rot = pltpu.roll(xs, HEAD_DIM // 2, 1)
            o_ref[:, hd * HEAD_DIM:(hd + 1) * HEAD_DIM] = ((xs * cos + rot * sin) * scale).astype(BF16)

    @pl.when(j == 0)
    def _():
        o_ref[...] = (res * (DIFF_QK_DIM ** -0.5 * LOG2E)).astype(BF16)

    @pl.when(j == 3)
    def _():
        rotary(1.0)

    @pl.when(j == 4)
    def _():
        rotary(HEAD_DIM ** -0.5)

    @pl.when((j != 0) & (j != 3) & (j != 4))
    def _():
        o_ref[...] = res.astype(BF16)


def _inproj(x, shift, scale, norm_g, cos_t, sin_t, w_in, *, seq_len):
    n, d = x.shape
    gw = w_in.shape[1] // N_PROJ
    tm = PROJ_ROW_TILE
    spt = seq_len // tm
    seq = lambda i, j: ((i * tm) // seq_len, 0, 0)
    return pl.pallas_call(
        _inproj_kernel,
        out_shape=jax.ShapeDtypeStruct((N_PROJ, n, gw), BF16),
        grid=(n // tm, N_PROJ),
        in_specs=[pl.BlockSpec((tm, d), lambda i, j: (i, 0)),
                  pl.BlockSpec((None, 1, d), seq),
                  pl.BlockSpec((None, 1, d), seq),
                  pl.BlockSpec((1, d), lambda i, j: (0, 0)),
                  pl.BlockSpec((tm, HEAD_DIM), lambda i, j: (i % spt, 0)),
                  pl.BlockSpec((tm, HEAD_DIM), lambda i, j: (i % spt, 0)),
                  pl.BlockSpec((d, gw), lambda i, j: (0, j))],
        out_specs=pl.BlockSpec((None, tm, gw), lambda i, j: (j, i, 0)),
        scratch_shapes=[pltpu.VMEM((tm, d), BF16)],
        compiler_params=_params(("parallel", "arbitrary")),
        name="inproj",
    )(x, shift, scale, norm_g, cos_t, sin_t, w_in)


def _t5_bucket(rel):
    half = N_BUCKETS // 2
    max_exact = half // 2
    ret = jnp.where(rel > 0, half, 0)
    n = jnp.abs(rel)
    nf = jnp.maximum(n, 1).astype(F32)
    large = max_exact + (jnp.log(nf / max_exact) / math.log(REL_MAX_DIST / max_exact)
                         * (half - max_exact)).astype(jnp.int32)
    large = jnp.minimum(large, half - 1)
    return ret + jnp.where(n < max_exact, n, large)


def _bias_kernel(rb_ref, bk_ref, o_ref):
    h = pl.program_id(0)
    for t in range(bk_ref.shape[0]):
        bk = bk_ref[t]
        acc = jnp.zeros(bk.shape, F32)
        for b in range(N_BUCKETS):
            acc = jnp.where(bk == b, rb_ref[b, h] * LOG2E, acc)
        o_ref[t] = acc


def _bias_tiles(rel_bias, t):
    assert t >= REL_MAX_DIST
    nh = rel_bias.shape[1]
    i = jnp.arange(t, dtype=jnp.int32)
    rel0 = i[:, None] - i[None, :]
    nt = 2 * BIAS_REACH + 1
    buckets = jnp.stack([_t5_bucket(rel0 + d * t) for d in range(-BIAS_REACH, BIAS_REACH + 1)])
    return pl.pallas_call(
        _bias_kernel,
        out_shape=jax.ShapeDtypeStruct((nh, nt, t, t), F32),
        grid=(nh,),
        in_specs=[pl.BlockSpec(memory_space=pltpu.SMEM),
                  pl.BlockSpec((nt, t, t), lambda h: (0, 0, 0))],
        out_specs=pl.BlockSpec((None, nt, t, t), lambda h: (h, 0, 0, 0)),
        compiler_params=_params(("parallel",)),
        name="bias_tiles",
    )(rel_bias, buckets)


def _attn_kernel(lamp_ref, q_ref, k_ref, v_ref, bias_ref, g_ref, o_ref,
                 vt_ref, s_ref, mt_ref, m_ref, acc_ref, *, t, seq_len, lam_init):
    qi = pl.program_id(2)
    nk = seq_len // t
    d = HEAD_DIM

    @pl.when(qi == 0)
    def _():
        for c in range(nk):
            vt_ref[:d, c * t:(c + 1) * t] = v_ref[c * t:(c + 1) * t, :].astype(F32).T.astype(BF16)
        vt_ref[d:, :] = jnp.ones((SUM_ROWS, seq_len), BF16)

    qt = q_ref[...].astype(F32).T
    row = lax.broadcasted_iota(jnp.int32, qt.shape, 0)
    qts = (jnp.where(row < DIFF_QK_DIM, qt, 0.0).astype(BF16), jnp.where(row >= DIFF_QK_DIM, qt, 0.0).astype(BF16))

    m_ref[...] = jnp.full(m_ref.shape, -jnp.inf, F32)
    acc_ref[...] = jnp.zeros(acc_ref.shape, F32)

    def scores(j, slot):
        kt = k_ref[pl.ds(pl.multiple_of(j * t, t), t), :]
        bias = bias_ref[jnp.clip(j - qi, -BIAS_REACH, BIAS_REACH) + BIAS_REACH]
        for idx in range(2):
            s = jnp.dot(kt, qts[idx], preferred_element_type=F32) + bias
            s_ref[slot, idx] = s
            mt_ref[slot, idx] = jnp.max(s, axis=0, keepdims=True)

    def consume(j, slot):
        vt = vt_ref[:, pl.ds(pl.multiple_of(j * t, t), t)]
        for idx in range(2):
            m_old = m_ref[idx]
            m_new = jnp.maximum(m_old, mt_ref[slot, idx])
            alpha = jnp.exp2(m_old - m_new)
            p = jnp.exp2(s_ref[slot, idx] - m_new)
            m_ref[idx] = m_new
            acc_ref[idx] = alpha * acc_ref[idx] + jnp.dot(vt, p.astype(BF16), preferred_element_type=F32)

    assert nk % 2 == 0
    scores(0, 0)

    def body(i, carry):
        j = 2 * i
        scores(j + 1, 1)
        consume(j, 0)
        scores(j + 2, 0)
        consume(j + 1, 1)
        return carry

    lax.fori_loop(0, nk // 2 - 1, body, 0)
    scores(nk - 1, 1)
    consume(nk - 2, 0)
    consume(nk - 1, 1)

    lp = lamp_ref[...]
    lam = (jnp.exp(jnp.sum(lp[0:1] * lp[1:2], axis=1, keepdims=True))
           - jnp.exp(jnp.sum(lp[2:3] * lp[3:4], axis=1, keepdims=True)) + lam_init)
    out = acc_ref[0, :d] / acc_ref[0, d:d + 1] - lam * (acc_ref[1, :d] / acc_ref[1, d:d + 1])
    y = out * lax.rsqrt(jnp.mean(out * out, axis=0, keepdims=True) + EPS)
    o_ref[...] = (y.T * g_ref[...] * (1.0 - lam_init)).astype(BF16)


def _diff_attention(proj, bias_tiles, lam_params, head_g, *, batch, seq_len, lam_init):
    n = proj.shape[1]
    nh = proj.shape[2] // HEAD_DIM
    t = ATTN_TILE
    nq = seq_len // t
    nt = bias_tiles.shape[1]
    kv = lambda g: (lambda h, b, qi: (g, b, h))
    return pl.pallas_call(
        functools.partial(_attn_kernel, t=t, seq_len=seq_len, lam_init=lam_init),
        out_shape=jax.ShapeDtypeStruct((n, nh * HEAD_DIM), BF16),
        grid=(nh, batch, nq),
        in_specs=[pl.BlockSpec(lam_params.shape, lambda h, b, qi: (0, 0)),
                  pl.BlockSpec((None, t, HEAD_DIM), lambda h, b, qi: (0, b * nq + qi, h)),
                  pl.BlockSpec((None, seq_len, HEAD_DIM), kv(1)),
                  pl.BlockSpec((None, seq_len, HEAD_DIM), kv(2)),
                  pl.BlockSpec((None, nt, t, t), lambda h, b, qi: (h, 0, 0, 0)),
                  pl.BlockSpec((1, HEAD_DIM), lambda h, b, qi: (0, 0))],
        out_specs=pl.BlockSpec((t, HEAD_DIM), lambda h, b, qi: (b * nq + qi, h)),
        scratch_shapes=[pltpu.VMEM((HEAD_DIM + SUM_ROWS, seq_len), BF16),
                        pltpu.VMEM((2, 2, t, t), F32), pltpu.VMEM((2, 2, 1, t), F32),
                        pltpu.VMEM((2, 1, t), F32),
                        pltpu.VMEM((2, HEAD_DIM + SUM_ROWS, t), F32)],
        compiler_params=_params(("parallel", "parallel", "arbitrary")),
        name="diff_attention",
    )(lam_params, proj, proj, proj, bias_tiles, head_g)


def _log_sigmoid(x):
    return jnp.minimum(x, 0.0) - jnp.log1p(jnp.exp(-jnp.abs(x)))


def _ret_kernel(df_ref, db_ref, q_ref, k_ref, v_ref, rg_ref, g_ref, o_ref, u_ref, r_ref, *, seq_len, c):
    h = pl.program_id(1)
    n = seq_len // c
    d = HEAD_DIM
    df, db = df_ref[0, h], db_ref[0, h]

    lgf = _log_sigmoid(jnp.full((c, d), df, F32))
    lgb = _log_sigmoid(jnp.full((c, d), db, F32))
    ri = lax.broadcasted_iota(jnp.int32, (c, d), 0).astype(F32)
    xi_f, ze_f = jnp.exp(lgf * (ri + 1.0)), jnp.exp(lgf * (c - 1.0 - ri))
    xi_b, ze_b = jnp.exp(lgb * (c - ri)), jnp.exp(lgb * ri)
    gc_f = jnp.exp(_log_sigmoid(jnp.full((d, d), df, F32)) * c)
    gc_b = jnp.exp(_log_sigmoid(jnp.full((d, d), db, F32)) * c)

    diff = (lax.broadcasted_iota(jnp.int32, (c, c), 0) - lax.broadcasted_iota(jnp.int32, (c, c), 1)).astype(F32)
    lgf2 = _log_sigmoid(jnp.full((c, c), df, F32))
    lgb2 = _log_sigmoid(jnp.full((c, c), db, F32))
    decay = jnp.where(diff >= 0, jnp.exp(lgf2 * jnp.maximum(diff, 0.0)), jnp.exp(lgb2 * jnp.maximum(-diff, 0.0)))

    def chunk(ref, i):
        return ref[pl.ds(pl.multiple_of(i * c, c), c), :]

    def local_state(i, carry):
        kc = chunk(k_ref, i).astype(F32)
        kz = jnp.concatenate([kc * ze_f, kc * ze_b], axis=1).astype(BF16)
        u_ref[i] = lax.dot_general(kz, chunk(v_ref, i), (((0,), (0,)), ((), ())), preferred_element_type=F32)
        return carry

    lax.fori_loop(0, n, local_state, 0)

    def scan_fwd(i, r):
        r_ref[i, :d, :] = r.astype(BF16)
        return gc_f * r + u_ref[i, :d, :]

    def scan_bwd(i, r):
        i = n - 1 - i
        r_ref[i, d:, :] = r.astype(BF16)
        return gc_b * r + u_ref[i, d:, :]

    lax.fori_loop(0, n, scan_fwd, jnp.zeros((d, d), F32))
    lax.fori_loop(0, n, scan_bwd, jnp.zeros((d, d), F32))

    def outputs(i, carry):
        qb = chunk(q_ref, i)
        qc = qb.astype(F32)
        vc = chunk(v_ref, i)
        qx = jnp.concatenate([qc * xi_f, qc * xi_b], axis=1).astype(BF16)
        inter = jnp.dot(qx, r_ref[i], preferred_element_type=F32)
        inner = lax.dot_general(qb, chunk(k_ref, i), (((1,), (1,)), ((), ())), preferred_element_type=F32) * decay
        o = inter + jnp.dot(inner.astype(BF16), vc, preferred_element_type=F32)
        rg = chunk(rg_ref, i).astype(F32)
        o_ref[pl.ds(pl.multiple_of(i * c, c), c), :] = (_rms_rows(o, g_ref[...]) * _silu(rg)).astype(BF16)
        return carry

    lax.fori_loop(0, n, outputs, 0)


def _retention(proj, decay_f, decay_b, head_g, *, batch, seq_len):
    n = proj.shape[1]
    nh = proj.shape[2] // HEAD_DIM
    c = RET_CHUNK
    blk = lambda g: pl.BlockSpec((None, seq_len, HEAD_DIM), lambda b, h: (g, b, h))
    smem = pl.BlockSpec(memory_space=pltpu.SMEM)
    return pl.pallas_call(
        functools.partial(_ret_kernel, seq_len=seq_len, c=c),
        out_shape=jax.ShapeDtypeStruct((n, nh * HEAD_DIM), BF16),
        grid=(batch, nh),
        in_specs=[smem, smem, blk(3), blk(4), blk(5), blk(6),
                  pl.BlockSpec((1, HEAD_DIM), lambda b, h: (0, 0))],
        out_specs=pl.BlockSpec((seq_len, HEAD_DIM), lambda b, h: (b, h)),
        scratch_shapes=[pltpu.VMEM((seq_len // c, 2 * HEAD_DIM, HEAD_DIM), F32),
                        pltpu.VMEM((seq_len // c, 2 * HEAD_DIM, HEAD_DIM), BF16)],
        compiler_params=_params(("parallel", "parallel")),
        name="retention",
    )(decay_f, decay_b, proj, proj, proj, proj, head_g)


def _outproj_kernel(x_ref, gt_ref, a_ref, r_ref, wa_ref, wr_ref, o_ref):
    y = (jnp.dot(a_ref[...], wa_ref[...], preferred_element_type=F32)
         + jnp.dot(r_ref[...], wr_ref[...], preferred_element_type=F32))
    o_ref[...] = x_ref[...] + gt_ref[...] * y


def _outproj(x, gate, d_out, r_out, w_out, *, seq_len):
    n, d = x.shape
    kw = d_out.shape[1]
    tm = PROJ_ROW_TILE
    row = lambda i: (i, 0)
    return pl.pallas_call(
        _outproj_kernel,
        out_shape=jax.ShapeDtypeStruct((n, d), F32),
        grid=(n // tm,),
        in_specs=[pl.BlockSpec((tm, d), row),
                  pl.BlockSpec((None, 1, d), lambda i: ((i * tm) // seq_len, 0, 0)),
                  pl.BlockSpec((tm, kw), row),
                  pl.BlockSpec((tm, kw), row),
                  pl.BlockSpec((kw, d), lambda i: (0, 0)),
                  pl.BlockSpec((kw, d), lambda i: (1, 0))],
        out_specs=pl.BlockSpec((tm, d), row),
        compiler_params=_params(("parallel",)),
        name="outproj",
    )(x, gate, d_out, r_out, w_out, w_out)


def _rotary_tables(seq_len):
    pos = jnp.arange(seq_len, dtype=F32)
    inv = ROPE_BASE ** (-jnp.arange(0, HEAD_DIM, 2, dtype=F32) / HEAD_DIM)
    ang = pos[:, None] * inv[None, :]
    cos, sin = jnp.cos(ang), jnp.sin(ang)
    return jnp.concatenate([cos, cos], axis=1), jnp.concatenate([-sin, sin], axis=1)


def _trunk(x, mod, w, bias_tiles):
    batch, seq_len, d = x.shape
    x = x.reshape(batch * seq_len, d)
    sh1, sc1, g1, shm, scm, gm, sh2, sc2, g2 = [mod[:, i].reshape(batch, 1, d) for i in range(N_MOD)]
    lam_init = 0.8 - 0.6 * math.exp(-0.3 * 0)
    cos_t, sin_t = _rotary_tables(seq_len)

    x = _ffn(x, sh1, sc1, g1, w["ffn1_norm_g"], w["final_norm_g"], w["ffn1_w13"], w["ffn1_w2"],
             seq_len=seq_len, final_norm=False)
    proj = _inproj(x, shm, scm, w["mix_norm_g"], cos_t, sin_t, w["w_in"], seq_len=seq_len)
    d_out = _diff_attention(proj, bias_tiles, w["lam_params"], w["diff_head_g"],
                            batch=batch, seq_len=seq_len, lam_init=lam_init)
    r_out = _retention(proj, w["ret_decay_fwd"], w["ret_decay_bwd"], w["ret_head_g"],
                       batch=batch, seq_len=seq_len)
    x = _outproj(x, gm, d_out, r_out, w["w_out"], seq_len=seq_len)
    x = _ffn(x, sh2, sc2, g2, w["ffn2_norm_g"], w["final_norm_g"], w["ffn2_w13"], w["ffn2_w2"],
             seq_len=seq_len, final_norm=True)
    return x.reshape(batch, seq_len, d)


def kernel(x_prompt, x_sample, c_prompt, c_sample, ada_w, ada_b, ffn1_norm_g, ffn1_w13, ffn1_w2, mix_norm_g, w_in, diff_lambda_q1, diff_lambda_k1, diff_lambda_q2, diff_lambda_k2, diff_head_g, rel_bias, ret_decay_fwd, ret_decay_bwd, ret_head_g, w_out, ffn2_norm_g, ffn2_w13, ffn2_w2, final_norm_g):
    assert ada_w.shape[0] == 1, "single-layer trunk"
    d = x_prompt.shape[-1]
    nb = c_prompt.shape[0]
    row = lambda a: a.reshape(1, -1)
    w = {
        "ffn1_norm_g": ffn1_norm_g, "mix_norm_g": mix_norm_g, "ffn2_norm_g": ffn2_norm_g,
        "final_norm_g": row(final_norm_g), "diff_head_g": diff_head_g, "ret_head_g": ret_head_g,
        "ffn1_w13": ffn1_w13[0].astype(BF16), "ffn1_w2": ffn1_w2[0].astype(BF16),
        "ffn2_w13": ffn2_w13[0].astype(BF16), "ffn2_w2": ffn2_w2[0].astype(BF16),
        "w_in": w_in[0].astype(BF16), "w_out": w_out[0].astype(BF16),
        "ret_decay_fwd": ret_decay_fwd, "ret_decay_bwd": ret_decay_bwd,
        "lam_params": jnp.concatenate([diff_lambda_q1, diff_lambda_k1, diff_lambda_q2, diff_lambda_k2], axis=0),
    }
    mod = _ada_mod(jnp.concatenate([c_prompt, c_sample], axis=0), ada_w[0], ada_b[0])
    mod = mod.reshape(mod.shape[0], N_MOD, d)
    bias_tiles = _bias_tiles(rel_bias, ATTN_TILE)
    y_prompt = _trunk(x_prompt, mod[:nb], w, bias_tiles)
    y_sample = _trunk(x_sample, mod[nb:], w, bias_tiles)
    return (y_prompt, y_sample)
```

```python
import functools
import math

import jax
import jax.numpy as jnp
from jax import lax
from jax.experimental import pallas as pl
from jax.experimental.pallas import tpu as pltpu

F32 = jnp.float32
BF16 = jnp.bfloat16

HEAD_DIM = 128
DIFF_QK_DIM = HEAD_DIM // 2
N_BUCKETS = 32
REL_MAX_DIST = 128
ROPE_BASE = 10000.0
N_MOD = 9
EPS = 1e-6
LOG2E = math.log2(math.e)
N_PROJ = 7

V7X_VMEM_BYTES = 64 * 1024 * 1024
VMEM_LIMIT = 56 * 1024 * 1024

FFN_ROW_TILE = 512
FFN_HID_TILE = 512
INPROJ_ROW_TILE = 256
PROJ_ROW_TILE = 512
NORM_ROW_BLOCK = 64
ATTN_TILE = 512
SUM_ROWS = 16
BIAS_REACH = 2
RET_CHUNK = 256
RET_UNROLL = 4
ADA_COL_TILE = 1024


def _params(sem):
    return pltpu.CompilerParams(dimension_semantics=sem, vmem_limit_bytes=VMEM_LIMIT)


def _rms_rows(x, g):
    return x * lax.rsqrt(jnp.mean(x * x, axis=-1, keepdims=True) + EPS) * g


def _silu(x):
    return x * jax.nn.sigmoid(x)


def _norm_modulate(x_ref, ng_ref, sc_ref, sh_ref, h_ref):
    gain = ng_ref[...] * (1.0 + sc_ref[...])
    shift = sh_ref[...]

    def block(i, carry):
        rows = pl.ds(pl.multiple_of(i * NORM_ROW_BLOCK, NORM_ROW_BLOCK), NORM_ROW_BLOCK)
        x = x_ref[rows, :]
        r = lax.rsqrt(jnp.mean(x * x, axis=-1, keepdims=True) + EPS)
        h_ref[rows, :] = (x * r * gain + shift).astype(BF16)
        return carry

    lax.fori_loop(0, x_ref.shape[0] // NORM_ROW_BLOCK, block, 0)


def _ada_kernel(c_ref, w_ref, b_ref, o_ref):
    a = _silu(c_ref[...]).astype(BF16)
    o_ref[...] = jnp.dot(a, w_ref[...].astype(BF16), preferred_element_type=F32) + b_ref[...]


def _ada_mod(c, w, b):
    nb, d = c.shape
    n = w.shape[1]
    tn = ADA_COL_TILE
    return pl.pallas_call(
        _ada_kernel,
        out_shape=jax.ShapeDtypeStruct((nb, n), F32),
        grid=(n // tn,),
        in_specs=[pl.BlockSpec((nb, d), lambda j: (0, 0)),
                  pl.BlockSpec((d, tn), lambda j: (0, j)),
                  pl.BlockSpec((1, tn), lambda j: (0, j))],
        out_specs=pl.BlockSpec((nb, tn), lambda j: (0, j)),
        compiler_params=_params(("parallel",)),
        name="ada_mod",
    )(c, w, b.reshape(1, n))


def _ffn_kernel(x_ref, sh_ref, sc_ref, gt_ref, ng_ref, fg_ref, w1_ref, w3_ref, w2_ref,
                o_ref, h_ref, *, final_norm):
    j = pl.program_id(1)

    @pl.when(j == 0)
    def _():
        _norm_modulate(x_ref, ng_ref, sc_ref, sh_ref, h_ref)
        o_ref[...] = jnp.zeros(o_ref.shape, F32)

    h = h_ref[...]
    g = jnp.dot(h, w1_ref[...], preferred_element_type=F32)
    u = jnp.dot(h, w3_ref[...], preferred_element_type=F32)
    o_ref[...] += jnp.dot((_silu(g) * u).astype(BF16), w2_ref[...], preferred_element_type=F32)

    @pl.when(j == pl.num_programs(1) - 1)
    def _():
        y = x_ref[...] + 0.5 * gt_ref[...] * o_ref[...]
        if final_norm:
            y = _rms_rows(y, fg_ref[...])
        o_ref[...] = y


def _col_blocks(w, width):
    k, n = w.shape
    return w.astype(BF16).reshape(k, n // width, width).transpose(1, 0, 2)


def _ffn(x, shift, scale, gate, norm_g, final_g, w13, w2, *, seq_len, final_norm):
    n, d = x.shape
    tm = FFN_ROW_TILE
    nf, tf = w13.shape[0] // 2, w13.shape[2]
    seq = lambda i, j: ((i * tm) // seq_len, 0, 0)
    row = lambda i, j: (i, 0)
    const = lambda i, j: (0, 0)
    return pl.pallas_call(
        functools.partial(_ffn_kernel, final_norm=final_norm),
        out_shape=jax.ShapeDtypeStruct((n, d), F32),
        grid=(n // tm, nf),
        in_specs=[pl.BlockSpec((tm, d), row),
                  pl.BlockSpec((None, 1, d), seq),
                  pl.BlockSpec((None, 1, d), seq),
                  pl.BlockSpec((None, 1, d), seq),
                  pl.BlockSpec((1, d), const),
                  pl.BlockSpec((1, d), const),
                  pl.BlockSpec((None, d, tf), lambda i, j: (j, 0, 0)),
                  pl.BlockSpec((None, d, tf), lambda i, j: (nf + j, 0, 0)),
                  pl.BlockSpec((tf, d), lambda i, j: (j, 0))],
        out_specs=pl.BlockSpec((tm, d), row),
        scratch_shapes=[pltpu.VMEM((tm, d), BF16)],
        compiler_params=_params(("parallel", "arbitrary")),
        name="ffn_final" if final_norm else "ffn",
    )(x, shift, scale, gate, norm_g, final_g, w13, w13, w2)


def _inproj_kernel(x_ref, sh_ref, sc_ref, ng_ref, cq_ref, sq_ref, w_ref, o_ref, h_ref):
    _norm_modulate(x_ref, ng_ref, sc_ref, sh_ref, h_ref)
    h = h_ref[...]
    cos, sin = cq_ref[...], sq_ref[...]
    for j in range(N_PROJ):
        res = jnp.dot(h, w_ref[j], preferred_element_type=F32)
        if j == 0:
            o_ref[j] = (res * (DIFF_QK_DIM ** -0.5 * LOG2E)).astype(BF16)
        elif j in (3, 4):
            scale = 1.0 if j == 3 else HEAD_DIM ** -0.5
            for hd in range(res.shape[1] // HEAD_DIM):
                xs = res[:, hd * HEAD_DIM:(hd + 1) * HEAD_DIM]
                rot = pltpu.roll(xs, HEAD_DIM // 2, 1)
                o_ref[j, :, hd * HEAD_DIM:(hd + 1) * HEAD_DIM] = ((xs * cos + rot * sin) * scale).astype(BF16)
        else:
            o_ref[j] = res.astype(BF16)


def _inproj(x, shift, scale, norm_g, cos_t, sin_t, w_in, *, seq_len):
    n, d = x.shape
    gw = w_in.shape[2]
    tm = INPROJ_ROW_TILE
    spt = seq_len // tm
    seq = lambda i: ((i * tm) // seq_len, 0, 0)
    return pl.pallas_call(
        _inproj_kernel,
        out_shape=jax.ShapeDtypeStruct((N_PROJ, n, gw), BF16),
        grid=(n // tm,),
        in_specs=[pl.BlockSpec((tm, d), lambda i: (i, 0)),
                  pl.BlockSpec((None, 1, d), seq),
                  pl.BlockSpec((None, 1, d), seq),
                  pl.BlockSpec((1, d), lambda i: (0, 0)),
                  pl.BlockSpec((tm, HEAD_DIM), lambda i: (i % spt, 0)),
                  pl.BlockSpec((tm, HEAD_DIM), lambda i: (i % spt, 0)),
                  pl.BlockSpec((N_PROJ, d, gw), lambda i: (0, 0, 0), pipeline_mode=pl.Buffered(1))],
        out_specs=pl.BlockSpec((N_PROJ, tm, gw), lambda i: (0, i, 0)),
        scratch_shapes=[pltpu.VMEM((tm, d), BF16)],
        compiler_params=_params(("parallel",)),
        name="inproj",
    )(x, shift, scale, norm_g, cos_t, sin_t, w_in)


def _t5_bucket(rel):
    half = N_BUCKETS // 2
    max_exact = half // 2
    ret = jnp.where(rel > 0, half, 0)
    n = jnp.abs(rel)
    nf = jnp.maximum(n, 1).astype(F32)
    large = max_exact + (jnp.log(nf / max_exact) / math.log(REL_MAX_DIST / max_exact)
                         * (half - max_exact)).astype(jnp.int32)
    large = jnp.minimum(large, half - 1)
    return ret + jnp.where(n < max_exact, n, large)


def _bias_kernel(rb_ref, bk_ref, o_ref):
    h = pl.program_id(0)
    half = N_BUCKETS // 2
    for i in range(o_ref.shape[0]):
        d = i - BIAS_REACH
        if abs(d) > 1:
            b = half - 1 if d < 0 else N_BUCKETS - 1
            o_ref[i] = jnp.full(o_ref.shape[1:], rb_ref[b, h] * LOG2E, F32)
            continue
        bk = bk_ref[d + 1]
        acc = jnp.zeros(bk.shape, F32)
        for b in (range(half) if d < 0 else range(half, N_BUCKETS) if d > 0 else range(N_BUCKETS)):
            acc = jnp.where(bk == b, rb_ref[b, h] * LOG2E, acc)
        o_ref[i] = acc


def _bias_tiles(rel_bias, t):
    assert t >= REL_MAX_DIST
    nh = rel_bias.shape[1]
    i = jnp.arange(t, dtype=jnp.int32)
    rel0 = i[:, None] - i[None, :]
    nt = 2 * BIAS_REACH + 1
    buckets = jnp.stack([_t5_bucket(rel0 + d * t) for d in (-1, 0, 1)])
    return pl.pallas_call(
        _bias_kernel,
        out_shape=jax.ShapeDtypeStruct((nh, nt, t, t), F32),
        grid=(nh,),
        in_specs=[pl.BlockSpec(memory_space=pltpu.SMEM),
                  pl.BlockSpec((3, t, t), lambda h: (0, 0, 0))],
        out_specs=pl.BlockSpec((None, nt, t, t), lambda h: (h, 0, 0, 0)),
        compiler_params=_params(("parallel",)),
        name="bias_tiles",
    )(rel_bias, buckets)


def _attn_kernel(lamp_ref, q_ref, k_ref, v_ref, bias_ref, g_ref, o_ref,
                 vt_ref, s_ref, mt_ref, m_ref, acc_ref, *, t, seq_len, lam_init):
    qi = pl.program_id(2)
    nk = seq_len // t
    d = HEAD_DIM

    @pl.when(qi == 0)
    def _():
        for c in range(nk):
            vt_ref[:d, c * t:(c + 1) * t] = v_ref[c * t:(c + 1) * t, :].astype(F32).T.astype(BF16)
        vt_ref[d:, :] = jnp.ones((SUM_ROWS, seq_len), BF16)

    qt = q_ref[...].astype(F32).T
    row = lax.broadcasted_iota(jnp.int32, qt.shape, 0)
    qts = (jnp.where(row < DIFF_QK_DIM, qt, 0.0).astype(BF16), jnp.where(row >= DIFF_QK_DIM, qt, 0.0).astype(BF16))

    m_ref[...] = jnp.full(m_ref.shape, -jnp.inf, F32)
    acc_ref[...] = jnp.zeros(acc_ref.shape, F32)

    def scores(j, slot):
        kt = k_ref[pl.ds(pl.multiple_of(j * t, t), t), :]
        bias = bias_ref[jnp.clip(j - qi, -BIAS_REACH, BIAS_REACH) + BIAS_REACH]
        for idx in range(2):
            s = jnp.dot(kt, qts[idx], preferred_element_type=F32) + bias
            s_ref[slot, idx] = s
            mt_ref[slot, idx] = jnp.max(s, axis=0, keepdims=True)

    def consume(j, slot):
        vt = vt_ref[:, pl.ds(pl.multiple_of(j * t, t), t)]
        for idx in range(2):
            m_old = m_ref[idx]
            m_new = jnp.maximum(m_old, mt_ref[slot, idx])
            alpha = jnp.exp2(m_old - m_new)
            p = jnp.exp2(s_ref[slot, idx] - m_new)
            m_ref[idx] = m_new
            acc_ref[idx] = alpha * acc_ref[idx] + jnp.dot(vt, p.astype(BF16), preferred_element_type=F32)

    assert nk % 2 == 0
    scores(0, 0)

    def body(i, carry):
        j = 2 * i
        scores(j + 1, 1)
        consume(j, 0)
        scores(j + 2, 0)
        consume(j + 1, 1)
        return carry

    lax.fori_loop(0, nk // 2 - 1, body, 0)
    scores(nk - 1, 1)
    consume(nk - 2, 0)
    consume(nk - 1, 1)

    lp = lamp_ref[...]
    lam = (jnp.exp(jnp.sum(lp[0:1] * lp[1:2], axis=1, keepdims=True))
           - jnp.exp(jnp.sum(lp[2:3] * lp[3:4], axis=1, keepdims=True)) + lam_init)
    out = acc_ref[0, :d] / acc_ref[0, d:d + 1] - lam * (acc_ref[1, :d] / acc_ref[1, d:d + 1])
    y = out * lax.rsqrt(jnp.mean(out * out, axis=0, keepdims=True) + EPS)
    o_ref[...] = (y.T * g_ref[...] * (1.0 - lam_init)).astype(BF16)


def _diff_attention(proj, bias_tiles, lam_params, head_g, *, batch, seq_len, lam_init):
    n = proj.shape[1]
    nh = proj.shape[2] // HEAD_DIM
    t = ATTN_TILE
    nq = seq_len // t
    nt = bias_tiles.shape[1]
    kv = lambda g: (lambda h, b, qi: (g, b, h))
    return pl.pallas_call(
        functools.partial(_attn_kernel, t=t, seq_len=seq_len, lam_init=lam_init),
        out_shape=jax.ShapeDtypeStruct((n, nh * HEAD_DIM), BF16),
        grid=(nh, batch, nq),
        in_specs=[pl.BlockSpec(lam_params.shape, lambda h, b, qi: (0, 0)),
                  pl.BlockSpec((None, t, HEAD_DIM), lambda h, b, qi: (0, b * nq + qi, h)),
                  pl.BlockSpec((None, seq_len, HEAD_DIM), kv(1)),
                  pl.BlockSpec((None, seq_len, HEAD_DIM), kv(2)),
                  pl.BlockSpec((None, nt, t, t), lambda h, b, qi: (h, 0, 0, 0)),
                  pl.BlockSpec((1, HEAD_DIM), lambda h, b, qi: (0, 0))],
        out_specs=pl.BlockSpec((t, HEAD_DIM), lambda h, b, qi: (b * nq + qi, h)),
        scratch_shapes=[pltpu.VMEM((HEAD_DIM + SUM_ROWS, seq_len), BF16),
                        pltpu.VMEM((2, 2, t, t), F32), pltpu.VMEM((2, 2, 1, t), F32),
                        pltpu.VMEM((2, 1, t), F32),
                        pltpu.VMEM((2, HEAD_DIM + SUM_ROWS, t), F32)],
        compiler_params=_params(("parallel", "parallel", "arbitrary")),
        name="diff_attention",
    )(lam_params, proj, proj, proj, bias_tiles, head_g)


def _log_sigmoid(x):
    return jnp.minimum(x, 0.0) - jnp.log1p(jnp.exp(-jnp.abs(x)))


def _ret_kernel(df_ref, db_ref, q_ref, k_ref, v_ref, rg_ref, g_ref, o_ref, u_ref, r_ref, *, seq_len, c):
    h = pl.program_id(1)
    n = seq_len // c
    d = HEAD_DIM
    df, db = df_ref[0, h], db_ref[0, h]

    lgf = _log_sigmoid(jnp.full((c, d), df, F32))
    lgb = _log_sigmoid(jnp.full((c, d), db, F32))
    ri = lax.broadcasted_iota(jnp.int32, (c, d), 0).astype(F32)
    xi_f, ze_f = jnp.exp(lgf * (ri + 1.0)), jnp.exp(lgf * (c - 1.0 - ri))
    xi_b, ze_b = jnp.exp(lgb * (c - ri)), jnp.exp(lgb * ri)
    gc_f = jnp.exp(_log_sigmoid(jnp.full((d, d), df, F32)) * c)
    gc_b = jnp.exp(_log_sigmoid(jnp.full((d, d), db, F32)) * c)

    diff = (lax.broadcasted_iota(jnp.int32, (c, c), 0) - lax.broadcasted_iota(jnp.int32, (c, c), 1)).astype(F32)
    lgf2 = _log_sigmoid(jnp.full((c, c), df, F32))
    lgb2 = _log_sigmoid(jnp.full((c, c), db, F32))
    decay = jnp.where(diff >= 0, jnp.exp(lgf2 * jnp.maximum(diff, 0.0)), jnp.exp(lgb2 * jnp.maximum(-diff, 0.0)))

    def chunk(ref, i):
        return ref[pl.ds(pl.multiple_of(i * c, c), c), :]

    def local_state(i, carry):
        kc = chunk(k_ref, i).astype(F32)
        kz = jnp.concatenate([kc * ze_f, kc * ze_b], axis=1).astype(BF16)
        u_ref[i] = lax.dot_general(kz, chunk(v_ref, i), (((0,), (0,)), ((), ())), preferred_element_type=F32)
        return carry

    lax.fori_loop(0, n, local_state, 0, unroll=RET_UNROLL)

    def scan_fwd(i, r):
        r_ref[i, :d, :] = r.astype(BF16)
        return gc_f * r + u_ref[i, :d, :]

    def scan_bwd(i, r):
        i = n - 1 - i
        r_ref[i, d:, :] = r.astype(BF16)
        return gc_b * r + u_ref[i, d:, :]

    lax.fori_loop(0, n, scan_fwd, jnp.zeros((d, d), F32))
    lax.fori_loop(0, n, scan_bwd, jnp.zeros((d, d), F32))

    def outputs(i, carry):
        qb = chunk(q_ref, i)
        qc = qb.astype(F32)
        vc = chunk(v_ref, i)
        qx = jnp.concatenate([qc * xi_f, qc * xi_b], axis=1).astype(BF16)
        inter = jnp.dot(qx, r_ref[i], preferred_element_type=F32)
        inner = lax.dot_general(qb, chunk(k_ref, i), (((1,), (1,)), ((), ())), preferred_element_type=F32) * decay
        o = inter + jnp.dot(inner.astype(BF16), vc, preferred_element_type=F32)
        rg = chunk(rg_ref, i).astype(F32)
        o_ref[pl.ds(pl.multiple_of(i * c, c), c), :] = (_rms_rows(o, g_ref[...]) * _silu(rg)).astype(BF16)
        return carry

    lax.fori_loop(0, n, outputs, 0, unroll=RET_UNROLL)


def _retention(proj, decay_f, decay_b, head_g, *, batch, seq_len):
    n = proj.shape[1]
    nh = proj.shape[2] // HEAD_DIM
    c = RET_CHUNK
    blk = lambda g: pl.BlockSpec((None, seq_len, HEAD_DIM), lambda b, h: (g, b, h))
    smem = pl.BlockSpec(memory_space=pltpu.SMEM)
    return pl.pallas_call(
        functools.partial(_ret_kernel, seq_len=seq_len, c=c),
        out_shape=jax.ShapeDtypeStruct((n, nh * HEAD_DIM), BF16),
        grid=(batch, nh),
        in_specs=[smem, smem, blk(3), blk(4), blk(5), blk(6),
                  pl.BlockSpec((1, HEAD_DIM), lambda b, h: (0, 0))],
        out_specs=pl.BlockSpec((seq_len, HEAD_DIM), lambda b, h: (b, h)),
        scratch_shapes=[pltpu.VMEM((seq_len // c, 2 * HEAD_DIM, HEAD_DIM), F32),
                        pltpu.VMEM((seq_len // c, 2 * HEAD_DIM, HEAD_DIM), BF16)],
        compiler_params=_params(("parallel", "parallel")),
        name="retention",
    )(decay_f, decay_b, proj, proj, proj, proj, head_g)


def _outproj_kernel(x_ref, gt_ref, a_ref, r_ref, wa_ref, wr_ref, o_ref):
    y = (jnp.dot(a_ref[...], wa_ref[...], preferred_element_type=F32)
         + jnp.dot(r_ref[...], wr_ref[...], preferred_element_type=F32))
    o_ref[...] = x_ref[...] + gt_ref[...] * y


def _outproj(x, gate, d_out, r_out, w_out, *, seq_len):
    n, d = x.shape
    kw = d_out.shape[1]
    tm = PROJ_ROW_TILE
    row = lambda i: (i, 0)
    return pl.pallas_call(
        _outproj_kernel,
        out_shape=jax.ShapeDtypeStruct((n, d), F32),
        grid=(n // tm,),
        in_specs=[pl.BlockSpec((tm, d), row),
                  pl.BlockSpec((None, 1, d), lambda i: ((i * tm) // seq_len, 0, 0)),
                  pl.BlockSpec((tm, kw), row),
                  pl.BlockSpec((tm, kw), row),
                  pl.BlockSpec((kw, d), lambda i: (0, 0)),
                  pl.BlockSpec((kw, d), lambda i: (1, 0))],
        out_specs=pl.BlockSpec((tm, d), row),
        compiler_params=_params(("parallel",)),
        name="outproj",
    )(x, gate, d_out, r_out, w_out, w_out)


def _rotary_tables(seq_len):
    pos = jnp.arange(seq_len, dtype=F32)
    inv = ROPE_BASE ** (-jnp.arange(0, HEAD_DIM, 2, dtype=F32) / HEAD_DIM)
    ang = pos[:, None] * inv[None, :]
    cos, sin = jnp.cos(ang), jnp.sin(ang)
    return jnp.concatenate([cos, cos], axis=1), jnp.concatenate([-sin, sin], axis=1)


def _trunk(x, mod, w, bias_tiles):
    batch, seq_len, d = x.shape
    x = x.reshape(batch * seq_len, d)
    sh1, sc1, g1, shm, scm, gm, sh2, sc2, g2 = [mod[:, i].reshape(batch, 1, d) for i in range(N_MOD)]
    lam_init = 0.8 - 0.6 * math.exp(-0.3 * 0)
    cos_t, sin_t = _rotary_tables(seq_len)

    x = _ffn(x, sh1, sc1, g1, w["ffn1_norm_g"], w["final_norm_g"], w["ffn1_w13"], w["ffn1_w2"],
             seq_len=seq_len, final_norm=False)
    proj = _inproj(x, shm, scm, w["mix_norm_g"], cos_t, sin_t, w["w_in"], seq_len=seq_len)
    d_out = _diff_attention(proj, bias_tiles, w["lam_params"], w["diff_head_g"],
                            batch=batch, seq_len=seq_len, lam_init=lam_init)
    r_out = _retention(proj, w["ret_decay_fwd"], w["ret_decay_bwd"], w["ret_head_g"],
                       batch=batch, seq_len=seq_len)
    x = _outproj(x, gm, d_out, r_out, w["w_out"], seq_len=seq_len)
    x = _ffn(x, sh2, sc2, g2, w["ffn2_norm_g"], w["final_norm_g"], w["ffn2_w13"], w["ffn2_w2"],
             seq_len=seq_len, final_norm=True)
    return x.reshape(batch, seq_len, d)


def kernel(x_prompt, x_sample, c_prompt, c_sample, ada_w, ada_b, ffn1_norm_g, ffn1_w13, ffn1_w2, mix_norm_g, w_in, diff_lambda_q1, diff_lambda_k1, diff_lambda_q2, diff_lambda_k2, diff_head_g, rel_bias, ret_decay_fwd, ret_decay_bwd, ret_head_g, w_out, ffn2_norm_g, ffn2_w13, ffn2_w2, final_norm_g):
    assert ada_w.shape[0] == 1, "single-layer trunk"
    d = x_prompt.shape[-1]
    nb = c_prompt.shape[0]
    row = lambda a: a.reshape(1, -1)
    w = {
        "ffn1_norm_g": ffn1_norm_g, "mix_norm_g": mix_norm_g, "ffn2_norm_g": ffn2_norm_g,
        "final_norm_g": row(final_norm_g), "diff_head_g": diff_head_g, "ret_head_g": ret_head_g,
        "ffn1_w13": _col_blocks(ffn1_w13[0], FFN_HID_TILE), "ffn1_w2": ffn1_w2[0].astype(BF16),
        "ffn2_w13": _col_blocks(ffn2_w13[0], FFN_HID_TILE), "ffn2_w2": ffn2_w2[0].astype(BF16),
        "w_in": _col_blocks(w_in[0], w_in.shape[2] // N_PROJ), "w_out": w_out[0].astype(BF16),
        "ret_decay_fwd": ret_decay_fwd, "ret_decay_bwd": ret_decay_bwd,
        "lam_params": jnp.concatenate([diff_lambda_q1, diff_lambda_k1, diff_lambda_q2, diff_lambda_k2], axis=0),
    }
    mod = _ada_mod(jnp.concatenate([c_prompt, c_sample], axis=0), ada_w[0], ada_b[0])
    mod = mod.reshape(mod.shape[0], N_MOD, d)
    bias_tiles = _bias_tiles(rel_bias, ATTN_TILE)
    y_prompt = _trunk(x_prompt, mod[:nb], w, bias_tiles)
    y_sample = _trunk(x_sample, mod[nb:], w, bias_tiles)
    return (y_prompt, y_sample)
```

```python
import functools
import math

import jax
import jax.numpy as jnp
from jax import lax
from jax.experimental import pallas as pl
from jax.experimental.pallas import tpu as pltpu

F32 = jnp.float32
BF16 = jnp.bfloat16

HEAD_DIM = 128
DIFF_QK_DIM = HEAD_DIM // 2
N_BUCKETS = 32
REL_MAX_DIST = 128
ROPE_BASE = 10000.0
N_MOD = 9
EPS = 1e-6
LOG2E = math.log2(math.e)
N_PROJ = 7

V7X_VMEM_BYTES = 64 * 1024 * 1024
VMEM_LIMIT = 56 * 1024 * 1024

FFN_ROW_TILE = 1024
FFN_HID_TILE = 512
INPROJ_ROW_TILE = 256
PROJ_ROW_TILE = 512
NORM_ROW_BLOCK = 64
ATTN_TILE = 512
SUM_ROWS = 16
BIAS_REACH = 2
RET_CHUNK = 256
RET_UNROLL = 4
ADA_COL_TILE = 1024


def _params(sem):
    return pltpu.CompilerParams(dimension_semantics=sem, vmem_limit_bytes=VMEM_LIMIT)


def _rms_rows(x, g):
    return x * lax.rsqrt(jnp.mean(x * x, axis=-1, keepdims=True) + EPS) * g


def _silu(x):
    return x * jax.nn.sigmoid(x)


def _norm_modulate(x_ref, ng_ref, sc_ref, sh_ref, h_ref):
    gain = ng_ref[...] * (1.0 + sc_ref[...])
    shift = sh_ref[...]

    def block(i, carry):
        rows = pl.ds(pl.multiple_of(i * NORM_ROW_BLOCK, NORM_ROW_BLOCK), NORM_ROW_BLOCK)
        x = x_ref[rows, :]
        r = lax.rsqrt(jnp.mean(x * x, axis=-1, keepdims=True) + EPS)
        h_ref[rows, :] = (x * r * gain + shift).astype(BF16)
        return carry

    lax.fori_loop(0, x_ref.shape[0] // NORM_ROW_BLOCK, block, 0)


def _ada_kernel(c_ref, w_ref, b_ref, o_ref):
    a = _silu(c_ref[...]).astype(BF16)
    o_ref[...] = jnp.dot(a, w_ref[...].astype(BF16), preferred_element_type=F32) + b_ref[...]


def _ada_mod(c, w, b):
    nb, d = c.shape
    n = w.shape[1]
    tn = ADA_COL_TILE
    return pl.pallas_call(
        _ada_kernel,
        out_shape=jax.ShapeDtypeStruct((nb, n), F32),
        grid=(n // tn,),
        in_specs=[pl.BlockSpec((nb, d), lambda j: (0, 0)),
                  pl.BlockSpec((d, tn), lambda j: (0, j)),
                  pl.BlockSpec((1, tn), lambda j: (0, j))],
        out_specs=pl.BlockSpec((nb, tn), lambda j: (0, j)),
        compiler_params=_params(("parallel",)),
        name="ada_mod",
    )(c, w, b.reshape(1, n))


def _ffn_kernel(x_ref, sh_ref, sc_ref, gt_ref, ng_ref, fg_ref, w1_ref, w3_ref, w2_ref,
                o_ref, h_ref, *, final_norm):
    j = pl.program_id(1)

    @pl.when(j == 0)
    def _():
        _norm_modulate(x_ref, ng_ref, sc_ref, sh_ref, h_ref)
        o_ref[...] = jnp.zeros(o_ref.shape, F32)

    h = h_ref[...]
    g = jnp.dot(h, w1_ref[...], preferred_element_type=F32)
    u = jnp.dot(h, w3_ref[...], preferred_element_type=F32)
    o_ref[...] += jnp.dot((_silu(g) * u).astype(BF16), w2_ref[...], preferred_element_type=F32)

    @pl.when(j == pl.num_programs(1) - 1)
    def _():
        y = x_ref[...] + 0.5 * gt_ref[...] * o_ref[...]
        if final_norm:
            y = _rms_rows(y, fg_ref[...])
        o_ref[...] = y


def _ffn(x, shift, scale, gate, norm_g, final_g, w13, w2, *, seq_len, final_norm):
    n, d = x.shape
    tm, tf = FFN_ROW_TILE, FFN_HID_TILE
    nf = w2.shape[0] // tf
    seq = lambda i, j: ((i * tm) // seq_len, 0, 0)
    row = lambda i, j: (i, 0)
    const = lambda i, j: (0, 0)
    return pl.pallas_call(
        functools.partial(_ffn_kernel, final_norm=final_norm),
        out_shape=jax.ShapeDtypeStruct((n, d), F32),
        grid=(n // tm, nf),
        in_specs=[pl.BlockSpec((tm, d), row, pipeline_mode=pl.Buffered(1)),
                  pl.BlockSpec((None, 1, d), seq),
                  pl.BlockSpec((None, 1, d), seq),
                  pl.BlockSpec((None, 1, d), seq),
                  pl.BlockSpec((1, d), const),
                  pl.BlockSpec((1, d), const),
                  pl.BlockSpec((d, tf), lambda i, j: (0, j)),
                  pl.BlockSpec((d, tf), lambda i, j: (0, nf + j)),
                  pl.BlockSpec((tf, d), lambda i, j: (j, 0))],
        out_specs=pl.BlockSpec((tm, d), row),
        scratch_shapes=[pltpu.VMEM((tm, d), BF16)],
        compiler_params=_params(("parallel", "arbitrary")),
        name="ffn_final" if final_norm else "ffn",
    )(x, shift, scale, gate, norm_g, final_g, w13, w13, w2)


def _inproj_kernel(x_ref, sh_ref, sc_ref, ng_ref, cq_ref, sq_ref, w_ref, o_ref, h_ref):
    _norm_modulate(x_ref, ng_ref, sc_ref, sh_ref, h_ref)
    h = h_ref[...]
    cos, sin = cq_ref[...], sq_ref[...]
    gw = w_ref.shape[1] // N_PROJ
    for j in range(N_PROJ):
        res = jnp.dot(h, w_ref[:, j * gw:(j + 1) * gw], preferred_element_type=F32)
        if j == 0:
            o_ref[j] = (res * (DIFF_QK_DIM ** -0.5 * LOG2E)).astype(BF16)
        elif j in (3, 4):
            scale = 1.0 if j == 3 else HEAD_DIM ** -0.5
            for hd in range(res.shape[1] // HEAD_DIM):
                xs = res[:, hd * HEAD_DIM:(hd + 1) * HEAD_DIM]
                rot = pltpu.roll(xs, HEAD_DIM // 2, 1)
                o_ref[j, :, hd * HEAD_DIM:(hd + 1) * HEAD_DIM] = ((xs * cos + rot * sin) * scale).astype(BF16)
        else:
            o_ref[j] = res.astype(BF16)


def _inproj(x, shift, scale, norm_g, cos_t, sin_t, w_in, *, seq_len):
    n, d = x.shape
    gw = w_in.shape[1] // N_PROJ
    tm = INPROJ_ROW_TILE
    spt = seq_len // tm
    seq = lambda i: ((i * tm) // seq_len, 0, 0)
    return pl.pallas_call(
        _inproj_kernel,
        out_shape=jax.ShapeDtypeStruct((N_PROJ, n, gw), BF16),
        grid=(n // tm,),
        in_specs=[pl.BlockSpec((tm, d), lambda i: (i, 0)),
                  pl.BlockSpec((None, 1, d), seq),
                  pl.BlockSpec((None, 1, d), seq),
                  pl.BlockSpec((1, d), lambda i: (0, 0)),
                  pl.BlockSpec((tm, HEAD_DIM), lambda i: (i % spt, 0)),
                  pl.BlockSpec((tm, HEAD_DIM), lambda i: (i % spt, 0)),
                  pl.BlockSpec(w_in.shape, lambda i: (0, 0), pipeline_mode=pl.Buffered(1))],
        out_specs=pl.BlockSpec((N_PROJ, tm, gw), lambda i: (0, i, 0)),
        scratch_shapes=[pltpu.VMEM((tm, d), BF16)],
        compiler_params=_params(("parallel",)),
        name="inproj",
    )(x, shift, scale, norm_g, cos_t, sin_t, w_in)


def _t5_bucket(rel):
    half = N_BUCKETS // 2
    max_exact = half // 2
    ret = jnp.where(rel > 0, half, 0)
    n = jnp.abs(rel)
    nf = jnp.maximum(n, 1).astype(F32)
    large = max_exact + (jnp.log(nf / max_exact) / math.log(REL_MAX_DIST / max_exact)
                         * (half - max_exact)).astype(jnp.int32)
    large = jnp.minimum(large, half - 1)
    return ret + jnp.where(n < max_exact, n, large)


def _bias_kernel(rb_ref, bk_ref, o_ref):
    h = pl.program_id(0)
    half = N_BUCKETS // 2
    for i in range(o_ref.shape[0]):
        d = i - BIAS_REACH
        if abs(d) > 1:
            b = half - 1 if d < 0 else N_BUCKETS - 1
            o_ref[i] = jnp.full(o_ref.shape[1:], rb_ref[b, h] * LOG2E, F32)
            continue
        bk = bk_ref[d + 1]
        acc = jnp.zeros(bk.shape, F32)
        for b in (range(half) if d < 0 else range(half, N_BUCKETS) if d > 0 else range(N_BUCKETS)):
            acc = jnp.where(bk == b, rb_ref[b, h] * LOG2E, acc)
        o_ref[i] = acc


def _bias_tiles(rel_bias, t):
    assert t >= REL_MAX_DIST
    nh = rel_bias.shape[1]
    i = jnp.arange(t, dtype=jnp.int32)
    rel0 = i[:, None] - i[None, :]
    nt = 2 * BIAS_REACH + 1
    buckets = jnp.stack([_t5_bucket(rel0 + d * t) for d in (-1, 0, 1)])
    return pl.pallas_call(
        _bias_kernel,
        out_shape=jax.ShapeDtypeStruct((nh, nt, t, t), F32),
        grid=(nh,),
        in_specs=[pl.BlockSpec(memory_space=pltpu.SMEM),
                  pl.BlockSpec((3, t, t), lambda h: (0, 0, 0))],
        out_specs=pl.BlockSpec((None, nt, t, t), lambda h: (h, 0, 0, 0)),
        compiler_params=_params(("parallel",)),
        name="bias_tiles",
    )(rel_bias, buckets)


def _attn_kernel(lamp_ref, q_ref, k_ref, v_ref, bias_ref, g_ref, o_ref,
                 vt_ref, s_ref, mt_ref, m_ref, acc_ref, *, t, seq_len, lam_init):
    qi = pl.program_id(2)
    nk = seq_len // t
    d = HEAD_DIM

    @pl.when(qi == 0)
    def _():
        for c in range(nk):
            vt_ref[:d, c * t:(c + 1) * t] = v_ref[c * t:(c + 1) * t, :].astype(F32).T.astype(BF16)
        vt_ref[d:, :] = jnp.ones((SUM_ROWS, seq_len), BF16)

    qt = q_ref[...].astype(F32).T
    row = lax.broadcasted_iota(jnp.int32, qt.shape, 0)
    qts = (jnp.where(row < DIFF_QK_DIM, qt, 0.0).astype(BF16), jnp.where(row >= DIFF_QK_DIM, qt, 0.0).astype(BF16))

    m_ref[...] = jnp.full(m_ref.shape, -jnp.inf, F32)
    acc_ref[...] = jnp.zeros(acc_ref.shape, F32)

    def scores(j, slot):
        kt = k_ref[pl.ds(pl.multiple_of(j * t, t), t), :]
        bias = bias_ref[jnp.clip(j - qi, -BIAS_REACH, BIAS_REACH) + BIAS_REACH]
        for idx in range(2):
            s = jnp.dot(kt, qts[idx], preferred_element_type=F32) + bias
            s_ref[slot, idx] = s
            mt_ref[slot, idx] = jnp.max(s, axis=0, keepdims=True)

    def consume(j, slot):
        vt = vt_ref[:, pl.ds(pl.multiple_of(j * t, t), t)]
        for idx in range(2):
            m_old = m_ref[idx]
            m_new = jnp.maximum(m_old, mt_ref[slot, idx])
            alpha = jnp.exp2(m_old - m_new)
            p = jnp.exp2(s_ref[slot, idx] - m_new)
            m_ref[idx] = m_new
            acc_ref[idx] = alpha * acc_ref[idx] + jnp.dot(vt, p.astype(BF16), preferred_element_type=F32)

    assert nk % 2 == 0
    scores(0, 0)

    def body(i, carry):
        j = 2 * i
        scores(j + 1, 1)
        consume(j, 0)
        scores(j + 2, 0)
        consume(j + 1, 1)
        return carry

    lax.fori_loop(0, nk // 2 - 1, body, 0)
    scores(nk - 1, 1)
    consume(nk - 2, 0)
    consume(nk - 1, 1)

    lp = lamp_ref[...]
    lam = (jnp.exp(jnp.sum(lp[0:1] * lp[1:2], axis=1, keepdims=True))
           - jnp.exp(jnp.sum(lp[2:3] * lp[3:4], axis=1, keepdims=True)) + lam_init)
    out = acc_ref[0, :d] / acc_ref[0, d:d + 1] - lam * (acc_ref[1, :d] / acc_ref[1, d:d + 1])
    y = out * lax.rsqrt(jnp.mean(out * out, axis=0, keepdims=True) + EPS)
    o_ref[...] = (y.T * g_ref[...] * (1.0 - lam_init)).astype(BF16)


def _diff_attention(proj, bias_tiles, lam_params, head_g, *, batch, seq_len, lam_init):
    n = proj.shape[1]
    nh = proj.shape[2] // HEAD_DIM
    t = ATTN_TILE
    nq = seq_len // t
    nt = bias_tiles.shape[1]
    kv = lambda g: (lambda h, b, qi: (g, b, h))
    return pl.pallas_call(
        functools.partial(_attn_kernel, t=t, seq_len=seq_len, lam_init=lam_init),
        out_shape=jax.ShapeDtypeStruct((n, nh * HEAD_DIM), BF16),
        grid=(nh, batch, nq),
        in_specs=[pl.BlockSpec(lam_params.shape, lambda h, b, qi: (0, 0)),
                  pl.BlockSpec((None, t, HEAD_DIM), lambda h, b, qi: (0, b * nq + qi, h)),
                  pl.BlockSpec((None, seq_len, HEAD_DIM), kv(1)),
                  pl.BlockSpec((None, seq_len, HEAD_DIM), kv(2)),
                  pl.BlockSpec((None, nt, t, t), lambda h, b, qi: (h, 0, 0, 0)),
                  pl.BlockSpec((1, HEAD_DIM), lambda h, b, qi: (0, 0))],
        out_specs=pl.BlockSpec((t, HEAD_DIM), lambda h, b, qi: (b * nq + qi, h)),
        scratch_shapes=[pltpu.VMEM((HEAD_DIM + SUM_ROWS, seq_len), BF16),
                        pltpu.VMEM((2, 2, t, t), F32), pltpu.VMEM((2, 2, 1, t), F32),
                        pltpu.VMEM((2, 1, t), F32),
                        pltpu.VMEM((2, HEAD_DIM + SUM_ROWS, t), F32)],
        compiler_params=_params(("parallel", "parallel", "arbitrary")),
        name="diff_attention",
    )(lam_params, proj, proj, proj, bias_tiles, head_g)


def _log_sigmoid(x):
    return jnp.minimum(x, 0.0) - jnp.log1p(jnp.exp(-jnp.abs(x)))


def _ret_kernel(df_ref, db_ref, q_ref, k_ref, v_ref, rg_ref, g_ref, o_ref, u_ref, r_ref, *, seq_len, c):
    h = pl.program_id(1)
    n = seq_len // c
    d = HEAD_DIM
    df, db = df_ref[0, h], db_ref[0, h]

    lgf = _log_sigmoid(jnp.full((c, d), df, F32))
    lgb = _log_sigmoid(jnp.full((c, d), db, F32))
    ri = lax.broadcasted_iota(jnp.int32, (c, d), 0).astype(F32)
    xi_f, ze_f = jnp.exp(lgf * (ri + 1.0)), jnp.exp(lgf * (c - 1.0 - ri))
    xi_b, ze_b = jnp.exp(lgb * (c - ri)), jnp.exp(lgb * ri)
    gc_f = jnp.exp(_log_sigmoid(jnp.full((d, d), df, F32)) * c)
    gc_b = jnp.exp(_log_sigmoid(jnp.full((d, d), db, F32)) * c)

    diff = (lax.broadcasted_iota(jnp.int32, (c, c), 0) - lax.broadcasted_iota(jnp.int32, (c, c), 1)).astype(F32)
    lgf2 = _log_sigmoid(jnp.full((c, c), df, F32))
    lgb2 = _log_sigmoid(jnp.full((c, c), db, F32))
    decay = jnp.where(diff >= 0, jnp.exp(lgf2 * jnp.maximum(diff, 0.0)), jnp.exp(lgb2 * jnp.maximum(-diff, 0.0)))

    def chunk(ref, i):
        return ref[pl.ds(pl.multiple_of(i * c, c), c), :]

    def local_state(i, carry):
        kc = chunk(k_ref, i).astype(F32)
        kz = jnp.concatenate([kc * ze_f, kc * ze_b], axis=1).astype(BF16)
        u_ref[i] = lax.dot_general(kz, chunk(v_ref, i), (((0,), (0,)), ((), ())), preferred_element_type=F32)
        return carry

    lax.fori_loop(0, n, local_state, 0, unroll=RET_UNROLL)

    def scan_fwd(i, r):
        r_ref[i, :d, :] = r.astype(BF16)
        return gc_f * r + u_ref[i, :d, :]

    def scan_bwd(i, r):
        i = n - 1 - i
        r_ref[i, d:, :] = r.astype(BF16)
        return gc_b * r + u_ref[i, d:, :]

    lax.fori_loop(0, n, scan_fwd, jnp.zeros((d, d), F32))
    lax.fori_loop(0, n, scan_bwd, jnp.zeros((d, d), F32))

    def outputs(i, carry):
        qb = chunk(q_ref, i)
        qc = qb.astype(F32)
        vc = chunk(v_ref, i)
        qx = jnp.concatenate([qc * xi_f, qc * xi_b], axis=1).astype(BF16)
        inter = jnp.dot(qx, r_ref[i], preferred_element_type=F32)
        inner = lax.dot_general(qb, chunk(k_ref, i), (((1,), (1,)), ((), ())), preferred_element_type=F32) * decay
        o = inter + jnp.dot(inner.astype(BF16), vc, preferred_element_type=F32)
        rg = chunk(rg_ref, i).astype(F32)
        o_ref[pl.ds(pl.multiple_of(i * c, c), c), :] = (_rms_rows(o, g_ref[...]) * _silu(rg)).astype(BF16)
        return carry

    lax.fori_loop(0, n, outputs, 0, unroll=RET_UNROLL)


def _retention(proj, decay_f, decay_b, head_g, *, batch, seq_len):
    n = proj.shape[1]
    nh = proj.shape[2] // HEAD_DIM
    c = RET_CHUNK
    blk = lambda g: pl.BlockSpec((None, seq_len, HEAD_DIM), lambda b, h: (g, b, h))
    smem = pl.BlockSpec(memory_space=pltpu.SMEM)
    return pl.pallas_call(
        functools.partial(_ret_kernel, seq_len=seq_len, c=c),
        out_shape=jax.ShapeDtypeStruct((n, nh * HEAD_DIM), BF16),
        grid=(batch, nh),
        in_specs=[smem, smem, blk(3), blk(4), blk(5), blk(6),
                  pl.BlockSpec((1, HEAD_DIM), lambda b, h: (0, 0))],
        out_specs=pl.BlockSpec((seq_len, HEAD_DIM), lambda b, h: (b, h)),
        scratch_shapes=[pltpu.VMEM((seq_len // c, 2 * HEAD_DIM, HEAD_DIM), F32),
                        pltpu.VMEM((seq_len // c, 2 * HEAD_DIM, HEAD_DIM), BF16)],
        compiler_params=_params(("parallel", "parallel")),
        name="retention",
    )(decay_f, decay_b, proj, proj, proj, proj, head_g)


def _outproj_kernel(x_ref, gt_ref, a_ref, r_ref, wa_ref, wr_ref, o_ref):
    y = (jnp.dot(a_ref[...], wa_ref[...], preferred_element_type=F32)
         + jnp.dot(r_ref[...], wr_ref[...], preferred_element_type=F32))
    o_ref[...] = x_ref[...] + gt_ref[...] * y


def _outproj(x, gate, d_out, r_out, w_out, *, seq_len):
    n, d = x.shape
    kw = d_out.shape[1]
    tm = PROJ_ROW_TILE
    row = lambda i: (i, 0)
    return pl.pallas_call(
        _outproj_kernel,
        out_shape=jax.ShapeDtypeStruct((n, d), F32),
        grid=(n // tm,),
        in_specs=[pl.BlockSpec((tm, d), row),
                  pl.BlockSpec((None, 1, d), lambda i: ((i * tm) // seq_len, 0, 0)),
                  pl.BlockSpec((tm, kw), row),
                  pl.BlockSpec((tm, kw), row),
                  pl.BlockSpec((kw, d), lambda i: (0, 0)),
                  pl.BlockSpec((kw, d), lambda i: (1, 0))],
        out_specs=pl.BlockSpec((tm, d), row),
        compiler_params=_params(("parallel",)),
        name="outproj",
    )(x, gate, d_out, r_out, w_out, w_out)


def _rotary_tables(seq_len):
    pos = jnp.arange(seq_len, dtype=F32)
    inv = ROPE_BASE ** (-jnp.arange(0, HEAD_DIM, 2, dtype=F32) / HEAD_DIM)
    ang = pos[:, None] * inv[None, :]
    cos, sin = jnp.cos(ang), jnp.sin(ang)
    return jnp.concatenate([cos, cos], axis=1), jnp.concatenate([-sin, sin], axis=1)


def _trunk(x, mod, w, bias_tiles):
    batch, seq_len, d = x.shape
    x = x.reshape(batch * seq_len, d)
    sh1, sc1, g1, shm, scm, gm, sh2, sc2, g2 = [mod[:, i].reshape(batch, 1, d) for i in range(N_MOD)]
    lam_init = 0.8 - 0.6 * math.exp(-0.3 * 0)
    cos_t, sin_t = _rotary_tables(seq_len)

    x = _ffn(x, sh1, sc1, g1, w["ffn1_norm_g"], w["final_norm_g"], w["ffn1_w13"], w["ffn1_w2"],
             seq_len=seq_len, final_norm=False)
    proj = _inproj(x, shm, scm, w["mix_norm_g"], cos_t, sin_t, w["w_in"], seq_len=seq_len)
    d_out = _diff_attention(proj, bias_tiles, w["lam_params"], w["diff_head_g"],
                            batch=batch, seq_len=seq_len, lam_init=lam_init)
    r_out = _retention(proj, w["ret_decay_fwd"], w["ret_decay_bwd"], w["ret_head_g"],
                       batch=batch, seq_len=seq_len)
    x = _outproj(x, gm, d_out, r_out, w["w_out"], seq_len=seq_len)
    x = _ffn(x, sh2, sc2, g2, w["ffn2_norm_g"], w["final_norm_g"], w["ffn2_w13"], w["ffn2_w2"],
             seq_len=seq_len, final_norm=True)
    return x.reshape(batch, seq_len, d)


def kernel(x_prompt, x_sample, c_prompt, c_sample, ada_w, ada_b, ffn1_norm_g, ffn1_w13, ffn1_w2, mix_norm_g, w_in, diff_lambda_q1, diff_lambda_k1, diff_lambda_q2, diff_lambda_k2, diff_head_g, rel_bias, ret_decay_fwd, ret_decay_bwd, ret_head_g, w_out, ffn2_norm_g, ffn2_w13, ffn2_w2, final_norm_g):
    assert ada_w.shape[0] == 1, "single-layer trunk"
    d = x_prompt.shape[-1]
    nb = c_prompt.shape[0]
    row = lambda a: a.reshape(1, -1)
    w = {
        "ffn1_norm_g": ffn1_norm_g, "mix_norm_g": mix_norm_g, "ffn2_norm_g": ffn2_norm_g,
        "final_norm_g": row(final_norm_g), "diff_head_g": diff_head_g, "ret_head_g": ret_head_g,
        "ffn1_w13": ffn1_w13[0].astype(BF16), "ffn1_w2": ffn1_w2[0].astype(BF16),
        "ffn2_w13": ffn2_w13[0].astype(BF16), "ffn2_w2": ffn2_w2[0].astype(BF16),
        "w_in": w_in[0].astype(BF16), "w_out": w_out[0].astype(BF16),
        "ret_decay_fwd": ret_decay_fwd, "ret_decay_bwd": ret_decay_bwd,
        "lam_params": jnp.concatenate([diff_lambda_q1, diff_lambda_k1, diff_lambda_q2, diff_lambda_k2], axis=0),
    }
    mod = _ada_mod(jnp.concatenate([c_prompt, c_sample], axis=0), ada_w[0], ada_b[0])
    mod = mod.reshape(mod.shape[0], N_MOD, d)
    bias_tiles = _bias_tiles(rel_bias, ATTN_TILE)
    y_prompt = _trunk(x_prompt, mod[:nb], w, bias_tiles)
    y_sample = _trunk(x_sample, mod[nb:], w, bias_tiles)
    return (y_prompt, y_sample)
```

```python
import functools
import math

import jax
import jax.numpy as jnp
from jax import lax
from jax.experimental import pallas as pl
from jax.experimental.pallas import tpu as pltpu

F32 = jnp.float32
BF16 = jnp.bfloat16

HEAD_DIM = 128
DIFF_QK_DIM = HEAD_DIM // 2
N_BUCKETS = 32
REL_MAX_DIST = 128
ROPE_BASE = 10000.0
N_MOD = 9
EPS = 1e-6
LOG2E = math.log2(math.e)
N_PROJ = 7

V7X_VMEM_BYTES = 64 * 1024 * 1024
VMEM_LIMIT = 56 * 1024 * 1024

FFN_ROW_TILE = 1024
FFN_HID_TILE = 512
INPROJ_ROW_TILE = 256
PROJ_ROW_TILE = 512
NORM_ROW_BLOCK = 64
ATTN_TILE = 512
ATTN_KEYS_PER_TRIP = 4
SUM_ROWS = 16
BIAS_REACH = 2
RET_CHUNK = 256
RET_UNROLL = 4
ADA_COL_TILE = 1024


def _params(sem):
    return pltpu.CompilerParams(dimension_semantics=sem, vmem_limit_bytes=VMEM_LIMIT)


def _rms_rows(x, g):
    return x * lax.rsqrt(jnp.mean(x * x, axis=-1, keepdims=True) + EPS) * g


def _silu(x):
    return x * jax.nn.sigmoid(x)


def _norm_modulate(x_ref, ng_ref, sc_ref, sh_ref, h_ref):
    gain = ng_ref[...] * (1.0 + sc_ref[...])
    shift = sh_ref[...]

    def block(i, carry):
        rows = pl.ds(pl.multiple_of(i * NORM_ROW_BLOCK, NORM_ROW_BLOCK), NORM_ROW_BLOCK)
        x = x_ref[rows, :]
        r = lax.rsqrt(jnp.mean(x * x, axis=-1, keepdims=True) + EPS)
        h_ref[rows, :] = (x * r * gain + shift).astype(BF16)
        return carry

    lax.fori_loop(0, x_ref.shape[0] // NORM_ROW_BLOCK, block, 0)


def _ada_kernel(c_ref, w_ref, b_ref, o_ref):
    a = _silu(c_ref[...]).astype(BF16)
    o_ref[...] = jnp.dot(a, w_ref[...].astype(BF16), preferred_element_type=F32) + b_ref[...]


def _ada_mod(c, w, b):
    nb, d = c.shape
    n = w.shape[1]
    tn = ADA_COL_TILE
    return pl.pallas_call(
        _ada_kernel,
        out_shape=jax.ShapeDtypeStruct((nb, n), F32),
        grid=(n // tn,),
        in_specs=[pl.BlockSpec((nb, d), lambda j: (0, 0)),
                  pl.BlockSpec((d, tn), lambda j: (0, j)),
                  pl.BlockSpec((1, tn), lambda j: (0, j))],
        out_specs=pl.BlockSpec((nb, tn), lambda j: (0, j)),
        compiler_params=_params(("parallel",)),
        name="ada_mod",
    )(c, w, b.reshape(1, n))


def _ffn_kernel(x_ref, sh_ref, sc_ref, gt_ref, ng_ref, fg_ref, w1_ref, w3_ref, w2_ref,
                o_ref, h_ref, *, final_norm):
    j = pl.program_id(1)

    @pl.when(j == 0)
    def _():
        _norm_modulate(x_ref, ng_ref, sc_ref, sh_ref, h_ref)
        o_ref[...] = jnp.zeros(o_ref.shape, F32)

    h = h_ref[...]
    g = jnp.dot(h, w1_ref[...], preferred_element_type=F32)
    u = jnp.dot(h, w3_ref[...], preferred_element_type=F32)
    o_ref[...] += jnp.dot((_silu(g) * u).astype(BF16), w2_ref[...], preferred_element_type=F32)

    @pl.when(j == pl.num_programs(1) - 1)
    def _():
        y = x_ref[...] + 0.5 * gt_ref[...] * o_ref[...]
        if final_norm:
            y = _rms_rows(y, fg_ref[...])
        o_ref[...] = y


def _ffn(x, shift, scale, gate, norm_g, final_g, w13, w2, *, seq_len, final_norm):
    n, d = x.shape
    tm, tf = FFN_ROW_TILE, FFN_HID_TILE
    nf = w2.shape[0] // tf
    seq = lambda i, j: ((i * tm) // seq_len, 0, 0)
    row = lambda i, j: (i, 0)
    const = lambda i, j: (0, 0)
    return pl.pallas_call(
        functools.partial(_ffn_kernel, final_norm=final_norm),
        out_shape=jax.ShapeDtypeStruct((n, d), F32),
        grid=(n // tm, nf),
        in_specs=[pl.BlockSpec((tm, d), row, pipeline_mode=pl.Buffered(1)),
                  pl.BlockSpec((None, 1, d), seq),
                  pl.BlockSpec((None, 1, d), seq),
                  pl.BlockSpec((None, 1, d), seq),
                  pl.BlockSpec((1, d), const),
                  pl.BlockSpec((1, d), const),
                  pl.BlockSpec((d, tf), lambda i, j: (0, j)),
                  pl.BlockSpec((d, tf), lambda i, j: (0, nf + j)),
                  pl.BlockSpec((tf, d), lambda i, j: (j, 0))],
        out_specs=pl.BlockSpec((tm, d), row),
        scratch_shapes=[pltpu.VMEM((tm, d), BF16)],
        compiler_params=_params(("parallel", "arbitrary")),
        name="ffn_final" if final_norm else "ffn",
    )(x, shift, scale, gate, norm_g, final_g, w13, w13, w2)


def _inproj_kernel(x_ref, sh_ref, sc_ref, ng_ref, cq_ref, sq_ref, w_ref, o_ref, h_ref):
    _norm_modulate(x_ref, ng_ref, sc_ref, sh_ref, h_ref)
    h = h_ref[...]
    cos, sin = cq_ref[...], sq_ref[...]
    gw = w_ref.shape[1] // N_PROJ
    for j in range(N_PROJ):
        res = jnp.dot(h, w_ref[:, j * gw:(j + 1) * gw], preferred_element_type=F32)
        if j == 0:
            o_ref[j] = (res * (DIFF_QK_DIM ** -0.5 * LOG2E)).astype(BF16)
        elif j in (3, 4):
            scale = 1.0 if j == 3 else HEAD_DIM ** -0.5
            for hd in range(res.shape[1] // HEAD_DIM):
                xs = res[:, hd * HEAD_DIM:(hd + 1) * HEAD_DIM]
                rot = pltpu.roll(xs, HEAD_DIM // 2, 1)
                o_ref[j, :, hd * HEAD_DIM:(hd + 1) * HEAD_DIM] = ((xs * cos + rot * sin) * scale).astype(BF16)
        else:
            o_ref[j] = res.astype(BF16)


def _inproj(x, shift, scale, norm_g, cos_t, sin_t, w_in, *, seq_len):
    n, d = x.shape
    gw = w_in.shape[1] // N_PROJ
    tm = INPROJ_ROW_TILE
    spt = seq_len // tm
    seq = lambda i: ((i * tm) // seq_len, 0, 0)
    return pl.pallas_call(
        _inproj_kernel,
        out_shape=jax.ShapeDtypeStruct((N_PROJ, n, gw), BF16),
        grid=(n // tm,),
        in_specs=[pl.BlockSpec((tm, d), lambda i: (i, 0)),
                  pl.BlockSpec((None, 1, d), seq),
                  pl.BlockSpec((None, 1, d), seq),
                  pl.BlockSpec((1, d), lambda i: (0, 0)),
                  pl.BlockSpec((tm, HEAD_DIM), lambda i: (i % spt, 0)),
                  pl.BlockSpec((tm, HEAD_DIM), lambda i: (i % spt, 0)),
                  pl.BlockSpec(w_in.shape, lambda i: (0, 0), pipeline_mode=pl.Buffered(1))],
        out_specs=pl.BlockSpec((N_PROJ, tm, gw), lambda i: (0, i, 0)),
        scratch_shapes=[pltpu.VMEM((tm, d), BF16)],
        compiler_params=_params(("parallel",)),
        name="inproj",
    )(x, shift, scale, norm_g, cos_t, sin_t, w_in)


def _t5_bucket(rel):
    half = N_BUCKETS // 2
    max_exact = half // 2
    ret = jnp.where(rel > 0, half, 0)
    n = jnp.abs(rel)
    nf = jnp.maximum(n, 1).astype(F32)
    large = max_exact + (jnp.log(nf / max_exact) / math.log(REL_MAX_DIST / max_exact)
                         * (half - max_exact)).astype(jnp.int32)
    large = jnp.minimum(large, half - 1)
    return ret + jnp.where(n < max_exact, n, large)


def _bias_kernel(rb_ref, bk_ref, o_ref):
    h = pl.program_id(0)
    half = N_BUCKETS // 2
    for i in range(o_ref.shape[0]):
        d = i - BIAS_REACH
        if abs(d) > 1:
            b = half - 1 if d < 0 else N_BUCKETS - 1
            o_ref[i] = jnp.full(o_ref.shape[1:], rb_ref[b, h] * LOG2E, F32)
            continue
        bk = bk_ref[d + 1]
        acc = jnp.zeros(bk.shape, F32)
        for b in (range(half) if d < 0 else range(half, N_BUCKETS) if d > 0 else range(N_BUCKETS)):
            acc = jnp.where(bk == b, rb_ref[b, h] * LOG2E, acc)
        o_ref[i] = acc


def _bias_tiles(rel_bias, t):
    assert t >= REL_MAX_DIST
    nh = rel_bias.shape[1]
    i = jnp.arange(t, dtype=jnp.int32)
    rel0 = i[:, None] - i[None, :]
    nt = 2 * BIAS_REACH + 1
    buckets = jnp.stack([_t5_bucket(rel0 + d * t) for d in (-1, 0, 1)])
    return pl.pallas_call(
        _bias_kernel,
        out_shape=jax.ShapeDtypeStruct((nh, nt, t, t), F32),
        grid=(nh,),
        in_specs=[pl.BlockSpec(memory_space=pltpu.SMEM),
                  pl.BlockSpec((3, t, t), lambda h: (0, 0, 0))],
        out_specs=pl.BlockSpec((None, nt, t, t), lambda h: (h, 0, 0, 0)),
        compiler_params=_params(("parallel",)),
        name="bias_tiles",
    )(rel_bias, buckets)


def _attn_kernel(lamp_ref, q_ref, k_ref, v_ref, bias_ref, g_ref, o_ref,
                 vt_ref, qt_ref, s_ref, mt_ref, m_ref, acc_ref, *, t, seq_len, lam_init):
    nk = seq_len // t
    d = HEAD_DIM
    assert ATTN_KEYS_PER_TRIP % 2 == 0 and nk % ATTN_KEYS_PER_TRIP == 0

    row = lax.broadcasted_iota(jnp.int32, (d, t), 0)
    for c in range(nk):
        cols = slice(c * t, (c + 1) * t)
        vt_ref[:d, cols] = v_ref[cols, :].astype(F32).T.astype(BF16)
        qt = q_ref[cols, :].astype(F32).T
        qt_ref[0, :, cols] = jnp.where(row < DIFF_QK_DIM, qt, 0.0).astype(BF16)
        qt_ref[1, :, cols] = jnp.where(row >= DIFF_QK_DIM, qt, 0.0).astype(BF16)
    vt_ref[d:, :] = jnp.ones((SUM_ROWS, seq_len), BF16)

    lp = lamp_ref[...]
    lam = (jnp.exp(jnp.sum(lp[0:1] * lp[1:2], axis=1, keepdims=True))
           - jnp.exp(jnp.sum(lp[2:3] * lp[3:4], axis=1, keepdims=True)) + lam_init)

    def scores(qi, j, slot):
        kt = k_ref[pl.ds(pl.multiple_of(j * t, t), t), :]
        bias = bias_ref[jnp.clip(j - qi, -BIAS_REACH, BIAS_REACH) + BIAS_REACH]
        for idx in range(2):
            qt = qt_ref[idx, :, pl.ds(pl.multiple_of(qi * t, t), t)]
            s = jnp.dot(kt, qt, preferred_element_type=F32) + bias
            s_ref[slot, idx] = s
            mt_ref[slot, idx] = jnp.max(s, axis=0, keepdims=True)

    def consume(j, slot):
        vt = vt_ref[:, pl.ds(pl.multiple_of(j * t, t), t)]
        for idx in range(2):
            m_old = m_ref[idx]
            m_new = jnp.maximum(m_old, mt_ref[slot, idx])
            alpha = jnp.exp2(m_old - m_new)
            p = jnp.exp2(s_ref[slot, idx] - m_new)
            m_ref[idx] = m_new
            acc_ref[idx] = alpha * acc_ref[idx] + jnp.dot(vt, p.astype(BF16), preferred_element_type=F32)

    scores(0, 0, 0)

    def query_tile(qi, carry):
        m_ref[...] = jnp.full(m_ref.shape, -jnp.inf, F32)
        acc_ref[...] = jnp.zeros(acc_ref.shape, F32)

        def key_group(i, c):
            for u in range(ATTN_KEYS_PER_TRIP):
                j = ATTN_KEYS_PER_TRIP * i + u
                wrap = j + 1 == nk
                scores(jnp.where(wrap, jnp.minimum(qi + 1, nk - 1), qi), jnp.where(wrap, 0, j + 1), (u + 1) % 2)
                consume(j, u % 2)
            return c

        lax.fori_loop(0, nk // ATTN_KEYS_PER_TRIP, key_group, 0)

        out = acc_ref[0, :d] / acc_ref[0, d:d + 1] - lam * (acc_ref[1, :d] / acc_ref[1, d:d + 1])
        y = out * lax.rsqrt(jnp.mean(out * out, axis=0, keepdims=True) + EPS)
        o_ref[pl.ds(pl.multiple_of(qi * t, t), t), :] = (y.T * g_ref[...] * (1.0 - lam_init)).astype(BF16)
        return carry

    lax.fori_loop(0, nk, query_tile, 0)


def _diff_attention(proj, bias_tiles, lam_params, head_g, *, batch, seq_len, lam_init):
    n = proj.shape[1]
    nh = proj.shape[2] // HEAD_DIM
    t = ATTN_TILE
    nt = bias_tiles.shape[1]
    blk = lambda g: pl.BlockSpec((None, seq_len, HEAD_DIM), lambda h, b: (g, b, h))
    return pl.pallas_call(
        functools.partial(_attn_kernel, t=t, seq_len=seq_len, lam_init=lam_init),
        out_shape=jax.ShapeDtypeStruct((n, nh * HEAD_DIM), BF16),
        grid=(nh, batch),
        in_specs=[pl.BlockSpec(lam_params.shape, lambda h, b: (0, 0)),
                  blk(0), blk(1), blk(2),
                  pl.BlockSpec((None, nt, t, t), lambda h, b: (h, 0, 0, 0)),
                  pl.BlockSpec((1, HEAD_DIM), lambda h, b: (0, 0))],
        out_specs=pl.BlockSpec((seq_len, HEAD_DIM), lambda h, b: (b, h)),
        scratch_shapes=[pltpu.VMEM((HEAD_DIM + SUM_ROWS, seq_len), BF16),
                        pltpu.VMEM((2, HEAD_DIM, seq_len), BF16),
                        pltpu.VMEM((2, 2, t, t), F32), pltpu.VMEM((2, 2, 1, t), F32),
                        pltpu.VMEM((2, 1, t), F32),
                        pltpu.VMEM((2, HEAD_DIM + SUM_ROWS, t), F32)],
        compiler_params=_params(("parallel", "parallel")),
        name="diff_attention",
    )(lam_params, proj, proj, proj, bias_tiles, head_g)


def _log_sigmoid(x):
    return jnp.minimum(x, 0.0) - jnp.log1p(jnp.exp(-jnp.abs(x)))


def _ret_kernel(df_ref, db_ref, q_ref, k_ref, v_ref, rg_ref, g_ref, o_ref, u_ref, r_ref, *, seq_len, c):
    h = pl.program_id(1)
    n = seq_len // c
    d = HEAD_DIM
    df, db = df_ref[0, h], db_ref[0, h]

    lgf = _log_sigmoid(jnp.full((c, d), df, F32))
    lgb = _log_sigmoid(jnp.full((c, d), db, F32))
    ri = lax.broadcasted_iota(jnp.int32, (c, d), 0).astype(F32)
    xi_f, ze_f = jnp.exp(lgf * (ri + 1.0)), jnp.exp(lgf * (c - 1.0 - ri))
    xi_b, ze_b = jnp.exp(lgb * (c - ri)), jnp.exp(lgb * ri)
    gc_f = jnp.exp(_log_sigmoid(jnp.full((d, d), df, F32)) * c)
    gc_b = jnp.exp(_log_sigmoid(jnp.full((d, d), db, F32)) * c)

    diff = (lax.broadcasted_iota(jnp.int32, (c, c), 0) - lax.broadcasted_iota(jnp.int32, (c, c), 1)).astype(F32)
    lgf2 = _log_sigmoid(jnp.full((c, c), df, F32))
    lgb2 = _log_sigmoid(jnp.full((c, c), db, F32))
    decay = jnp.where(diff >= 0, jnp.exp(lgf2 * jnp.maximum(diff, 0.0)), jnp.exp(lgb2 * jnp.maximum(-diff, 0.0)))

    def chunk(ref, i):
        return ref[pl.ds(pl.multiple_of(i * c, c), c), :]

    def local_state(i, carry):
        kc = chunk(k_ref, i).astype(F32)
        kz = jnp.concatenate([kc * ze_f, kc * ze_b], axis=1).astype(BF16)
        u_ref[i] = lax.dot_general(kz, chunk(v_ref, i), (((0,), (0,)), ((), ())), preferred_element_type=F32)
        return carry

    lax.fori_loop(0, n, local_state, 0, unroll=RET_UNROLL)

    def scan_fwd(i, r):
        r_ref[i, :d, :] = r.astype(BF16)
        return gc_f * r + u_ref[i, :d, :]

    def scan_bwd(i, r):
        i = n - 1 - i
        r_ref[i, d:, :] = r.astype(BF16)
        return gc_b * r + u_ref[i, d:, :]

    lax.fori_loop(0, n, scan_fwd, jnp.zeros((d, d), F32))
    lax.fori_loop(0, n, scan_bwd, jnp.zeros((d, d), F32))

    def outputs(i, carry):
        qb = chunk(q_ref, i)
        qc = qb.astype(F32)
        vc = chunk(v_ref, i)
        qx = jnp.concatenate([qc * xi_f, qc * xi_b], axis=1).astype(BF16)
        inter = jnp.dot(qx, r_ref[i], preferred_element_type=F32)
        inner = lax.dot_general(qb, chunk(k_ref, i), (((1,), (1,)), ((), ())), preferred_element_type=F32) * decay
        o = inter + jnp.dot(inner.astype(BF16), vc, preferred_element_type=F32)
        rg = chunk(rg_ref, i).astype(F32)
        o_ref[pl.ds(pl.multiple_of(i * c, c), c), :] = (_rms_rows(o, g_ref[...]) * _silu(rg)).astype(BF16)
        return carry

    lax.fori_loop(0, n, outputs, 0, unroll=RET_UNROLL)


def _retention(proj, decay_f, decay_b, head_g, *, batch, seq_len):
    n = proj.shape[1]
    nh = proj.shape[2] // HEAD_DIM
    c = RET_CHUNK
    blk = lambda g: pl.BlockSpec((None, seq_len, HEAD_DIM), lambda b, h: (g, b, h))
    smem = pl.BlockSpec(memory_space=pltpu.SMEM)
    return pl.pallas_call(
        functools.partial(_ret_kernel, seq_len=seq_len, c=c),
        out_shape=jax.ShapeDtypeStruct((n, nh * HEAD_DIM), BF16),
        grid=(batch, nh),
        in_specs=[smem, smem, blk(3), blk(4), blk(5), blk(6),
                  pl.BlockSpec((1, HEAD_DIM), lambda b, h: (0, 0))],
        out_specs=pl.BlockSpec((seq_len, HEAD_DIM), lambda b, h: (b, h)),
        scratch_shapes=[pltpu.VMEM((seq_len // c, 2 * HEAD_DIM, HEAD_DIM), F32),
                        pltpu.VMEM((seq_len // c, 2 * HEAD_DIM, HEAD_DIM), BF16)],
        compiler_params=_params(("parallel", "parallel")),
        name="retention",
    )(decay_f, decay_b, proj, proj, proj, proj, head_g)


def _outproj_kernel(x_ref, gt_ref, a_ref, r_ref, wa_ref, wr_ref, o_ref):
    y = (jnp.dot(a_ref[...], wa_ref[...], preferred_element_type=F32)
         + jnp.dot(r_ref[...], wr_ref[...], preferred_element_type=F32))
    o_ref[...] = x_ref[...] + gt_ref[...] * y


def _outproj(x, gate, d_out, r_out, w_out, *, seq_len):
    n, d = x.shape
    kw = d_out.shape[1]
    tm = PROJ_ROW_TILE
    row = lambda i: (i, 0)
    return pl.pallas_call(
        _outproj_kernel,
        out_shape=jax.ShapeDtypeStruct((n, d), F32),
        grid=(n // tm,),
        in_specs=[pl.BlockSpec((tm, d), row),
                  pl.BlockSpec((None, 1, d), lambda i: ((i * tm) // seq_len, 0, 0)),
                  pl.BlockSpec((tm, kw), row),
                  pl.BlockSpec((tm, kw), row),
                  pl.BlockSpec((kw, d), lambda i: (0, 0)),
                  pl.BlockSpec((kw, d), lambda i: (1, 0))],
        out_specs=pl.BlockSpec((tm, d), row),
        compiler_params=_params(("parallel",)),
        name="outproj",
    )(x, gate, d_out, r_out, w_out, w_out)


def _rotary_tables(seq_len):
    pos = jnp.arange(seq_len, dtype=F32)
    inv = ROPE_BASE ** (-jnp.arange(0, HEAD_DIM, 2, dtype=F32) / HEAD_DIM)
    ang = pos[:, None] * inv[None, :]
    cos, sin = jnp.cos(ang), jnp.sin(ang)
    return jnp.concatenate([cos, cos], axis=1), jnp.concatenate([-sin, sin], axis=1)


def _trunk(x, mod, w, bias_tiles):
    batch, seq_len, d = x.shape
    x = x.reshape(batch * seq_len, d)
    sh1, sc1, g1, shm, scm, gm, sh2, sc2, g2 = [mod[:, i].reshape(batch, 1, d) for i in range(N_MOD)]
    lam_init = 0.8 - 0.6 * math.exp(-0.3 * 0)
    cos_t, sin_t = _rotary_tables(seq_len)

    x = _ffn(x, sh1, sc1, g1, w["ffn1_norm_g"], w["final_norm_g"], w["ffn1_w13"], w["ffn1_w2"],
             seq_len=seq_len, final_norm=False)
    proj = _inproj(x, shm, scm, w["mix_norm_g"], cos_t, sin_t, w["w_in"], seq_len=seq_len)
    d_out = _diff_attention(proj, bias_tiles, w["lam_params"], w["diff_head_g"],
                            batch=batch, seq_len=seq_len, lam_init=lam_init)
    r_out = _retention(proj, w["ret_decay_fwd"], w["ret_decay_bwd"], w["ret_head_g"],
                       batch=batch, seq_len=seq_len)
    x = _outproj(x, gm, d_out, r_out, w["w_out"], seq_len=seq_len)
    x = _ffn(x, sh2, sc2, g2, w["ffn2_norm_g"], w["final_norm_g"], w["ffn2_w13"], w["ffn2_w2"],
             seq_len=seq_len, final_norm=True)
    return x.reshape(batch, seq_len, d)


def kernel(x_prompt, x_sample, c_prompt, c_sample, ada_w, ada_b, ffn1_norm_g, ffn1_w13, ffn1_w2, mix_norm_g, w_in, diff_lambda_q1, diff_lambda_k1, diff_lambda_q2, diff_lambda_k2, diff_head_g, rel_bias, ret_decay_fwd, ret_decay_bwd, ret_head_g, w_out, ffn2_norm_g, ffn2_w13, ffn2_w2, final_norm_g):
    assert ada_w.shape[0] == 1, "single-layer trunk"
    d = x_prompt.shape[-1]
    nb = c_prompt.shape[0]
    row = lambda a: a.reshape(1, -1)
    w = {
        "ffn1_norm_g": ffn1_norm_g, "mix_norm_g": mix_norm_g, "ffn2_norm_g": ffn2_norm_g,
        "final_norm_g": row(final_norm_g), "diff_head_g": diff_head_g, "ret_head_g": ret_head_g,
        "ffn1_w13": ffn1_w13[0].astype(BF16), "ffn1_w2": ffn1_w2[0].astype(BF16),
        "ffn2_w13": ffn2_w13[0].astype(BF16), "ffn2_w2": ffn2_w2[0].astype(BF16),
        "w_in": w_in[0].astype(BF16), "w_out": w_out[0].astype(BF16),
        "ret_decay_fwd": ret_decay_fwd, "ret_decay_bwd": ret_decay_bwd,
        "lam_params": jnp.concatenate([diff_lambda_q1, diff_lambda_k1, diff_lambda_q2, diff_lambda_k2], axis=0),
    }
    mod = _ada_mod(jnp.concatenate([c_prompt, c_sample], axis=0), ada_w[0], ada_b[0])
    mod = mod.reshape(mod.shape[0], N_MOD, d)
    bias_tiles = _bias_tiles(rel_bias, ATTN_TILE)
    y_prompt = _trunk(x_prompt, mod[:nb], w, bias_tiles)
    y_sample = _trunk(x_sample, mod[nb:], w, bias_tiles)
    return (y_prompt, y_sample)
```

```python
import functools
import math

import jax
import jax.numpy as jnp
from jax import lax
from jax.experimental import pallas as pl
from jax.experimental.pallas import tpu as pltpu

F32 = jnp.float32
BF16 = jnp.bfloat16

HEAD_DIM = 128
DIFF_QK_DIM = HEAD_DIM // 2
N_BUCKETS = 32
REL_MAX_DIST = 128
ROPE_BASE = 10000.0
N_MOD = 9
EPS = 1e-6
LOG2E = math.log2(math.e)
N_PROJ = 7

V7X_VMEM_BYTES = 64 * 1024 * 1024
VMEM_LIMIT = 56 * 1024 * 1024

FFN_ROW_TILE = 1024
FFN_HID_TILE = 512
INPROJ_ROW_TILE = 256
PROJ_ROW_TILE = 512
NORM_ROW_BLOCK = 64
NORM_UNROLL = 2
ATTN_TILE = 512
ATTN_KEYS_PER_TRIP = 4
SUM_ROWS = 16
BIAS_REACH = 2
RET_CHUNK = 256
RET_UNROLL = 8
ADA_COL_TILE = 1024


def _params(sem):
    return pltpu.CompilerParams(dimension_semantics=sem, vmem_limit_bytes=VMEM_LIMIT)


def _rms_rows(x, g):
    return x * lax.rsqrt(jnp.mean(x * x, axis=-1, keepdims=True) + EPS) * g


def _silu(x):
    return x * jax.nn.sigmoid(x)


def _norm_modulate(x_ref, ng_ref, sc_ref, sh_ref, h_ref):
    gain = ng_ref[...] * (1.0 + sc_ref[...])
    shift = sh_ref[...]

    def block(i, carry):
        rows = pl.ds(pl.multiple_of(i * NORM_ROW_BLOCK, NORM_ROW_BLOCK), NORM_ROW_BLOCK)
        x = x_ref[rows, :]
        r = lax.rsqrt(jnp.mean(x * x, axis=-1, keepdims=True) + EPS)
        h_ref[rows, :] = (x * r * gain + shift).astype(BF16)
        return carry

    lax.fori_loop(0, x_ref.shape[0] // NORM_ROW_BLOCK, block, 0, unroll=NORM_UNROLL)


def _ada_kernel(c_ref, w_ref, b_ref, o_ref):
    a = _silu(c_ref[...]).astype(BF16)
    o_ref[...] = jnp.dot(a, w_ref[...].astype(BF16), preferred_element_type=F32) + b_ref[...]


def _ada_mod(c, w, b):
    nb, d = c.shape
    n = w.shape[1]
    tn = ADA_COL_TILE
    return pl.pallas_call(
        _ada_kernel,
        out_shape=jax.ShapeDtypeStruct((nb, n), F32),
        grid=(n // tn,),
        in_specs=[pl.BlockSpec((nb, d), lambda j: (0, 0)),
                  pl.BlockSpec((d, tn), lambda j: (0, j)),
                  pl.BlockSpec((1, tn), lambda j: (0, j))],
        out_specs=pl.BlockSpec((nb, tn), lambda j: (0, j)),
        compiler_params=_params(("parallel",)),
        name="ada_mod",
    )(c, w, b.reshape(1, n))


def _ffn_kernel(x_ref, sh_ref, sc_ref, gt_ref, ng_ref, fg_ref, w1_ref, w3_ref, w2_ref,
                o_ref, h_ref, *, final_norm):
    j = pl.program_id(1)

    @pl.when(j == 0)
    def _():
        _norm_modulate(x_ref, ng_ref, sc_ref, sh_ref, h_ref)
        o_ref[...] = jnp.zeros(o_ref.shape, F32)

    h = h_ref[...]
    g = jnp.dot(h, w1_ref[...], preferred_element_type=F32)
    u = jnp.dot(h, w3_ref[...], preferred_element_type=F32)
    o_ref[...] += jnp.dot((_silu(g) * u).astype(BF16), w2_ref[...], preferred_element_type=F32)

    @pl.when(j == pl.num_programs(1) - 1)
    def _():
        y = x_ref[...] + 0.5 * gt_ref[...] * o_ref[...]
        if final_norm:
            y = _rms_rows(y, fg_ref[...])
        o_ref[...] = y


def _ffn(x, shift, scale, gate, norm_g, final_g, w13, w2, *, seq_len, final_norm):
    n, d = x.shape
    tm, tf = FFN_ROW_TILE, FFN_HID_TILE
    nf = w2.shape[0] // tf
    seq = lambda i, j: ((i * tm) // seq_len, 0, 0)
    row = lambda i, j: (i, 0)
    const = lambda i, j: (0, 0)
    return pl.pallas_call(
        functools.partial(_ffn_kernel, final_norm=final_norm),
        out_shape=jax.ShapeDtypeStruct((n, d), F32),
        grid=(n // tm, nf),
        in_specs=[pl.BlockSpec((tm, d), row, pipeline_mode=pl.Buffered(1)),
                  pl.BlockSpec((None, 1, d), seq),
                  pl.BlockSpec((None, 1, d), seq),
                  pl.BlockSpec((None, 1, d), seq),
                  pl.BlockSpec((1, d), const),
                  pl.BlockSpec((1, d), const),
                  pl.BlockSpec((d, tf), lambda i, j: (0, j)),
                  pl.BlockSpec((d, tf), lambda i, j: (0, nf + j)),
                  pl.BlockSpec((tf, d), lambda i, j: (j, 0))],
        out_specs=pl.BlockSpec((tm, d), row),
        scratch_shapes=[pltpu.VMEM((tm, d), BF16)],
        compiler_params=_params(("parallel", "arbitrary")),
        name="ffn_final" if final_norm else "ffn",
    )(x, shift, scale, gate, norm_g, final_g, w13, w13, w2)


def _inproj_kernel(x_ref, sh_ref, sc_ref, ng_ref, cq_ref, sq_ref, w_ref, o_ref, h_ref):
    _norm_modulate(x_ref, ng_ref, sc_ref, sh_ref, h_ref)
    h = h_ref[...]
    cos, sin = cq_ref[...], sq_ref[...]
    gw = w_ref.shape[1] // N_PROJ
    for j in range(N_PROJ):
        res = jnp.dot(h, w_ref[:, j * gw:(j + 1) * gw], preferred_element_type=F32)
        if j == 0:
            o_ref[j] = (res * (DIFF_QK_DIM ** -0.5 * LOG2E)).astype(BF16)
        elif j in (3, 4):
            scale = 1.0 if j == 3 else HEAD_DIM ** -0.5
            for hd in range(res.shape[1] // HEAD_DIM):
                xs = res[:, hd * HEAD_DIM:(hd + 1) * HEAD_DIM]
                rot = pltpu.roll(xs, HEAD_DIM // 2, 1)
                o_ref[j, :, hd * HEAD_DIM:(hd + 1) * HEAD_DIM] = ((xs * cos + rot * sin) * scale).astype(BF16)
        else:
            o_ref[j] = res.astype(BF16)


def _inproj(x, shift, scale, norm_g, cos_t, sin_t, w_in, *, seq_len):
    n, d = x.shape
    gw = w_in.shape[1] // N_PROJ
    tm = INPROJ_ROW_TILE
    spt = seq_len // tm
    seq = lambda i: ((i * tm) // seq_len, 0, 0)
    return pl.pallas_call(
        _inproj_kernel,
        out_shape=jax.ShapeDtypeStruct((N_PROJ, n, gw), BF16),
        grid=(n // tm,),
        in_specs=[pl.BlockSpec((tm, d), lambda i: (i, 0)),
                  pl.BlockSpec((None, 1, d), seq),
                  pl.BlockSpec((None, 1, d), seq),
                  pl.BlockSpec((1, d), lambda i: (0, 0)),
                  pl.BlockSpec((tm, HEAD_DIM), lambda i: (i % spt, 0)),
                  pl.BlockSpec((tm, HEAD_DIM), lambda i: (i % spt, 0)),
                  pl.BlockSpec(w_in.shape, lambda i: (0, 0), pipeline_mode=pl.Buffered(1))],
        out_specs=pl.BlockSpec((N_PROJ, tm, gw), lambda i: (0, i, 0)),
        scratch_shapes=[pltpu.VMEM((tm, d), BF16)],
        compiler_params=_params(("parallel",)),
        name="inproj",
    )(x, shift, scale, norm_g, cos_t, sin_t, w_in)


def _t5_bucket(rel):
    half = N_BUCKETS // 2
    max_exact = half // 2
    ret = jnp.where(rel > 0, half, 0)
    n = jnp.abs(rel)
    nf = jnp.maximum(n, 1).astype(F32)
    large = max_exact + (jnp.log(nf / max_exact) / math.log(REL_MAX_DIST / max_exact)
                         * (half - max_exact)).astype(jnp.int32)
    large = jnp.minimum(large, half - 1)
    return ret + jnp.where(n < max_exact, n, large)


def _bias_kernel(rb_ref, bk_ref, o_ref):
    h = pl.program_id(0)
    half = N_BUCKETS // 2
    for i in range(o_ref.shape[0]):
        d = i - BIAS_REACH
        if abs(d) > 1:
            b = half - 1 if d < 0 else N_BUCKETS - 1
            o_ref[i] = jnp.full(o_ref.shape[1:], rb_ref[b, h] * LOG2E, F32)
            continue
        bk = bk_ref[d + 1]
        acc = jnp.zeros(bk.shape, F32)
        for b in (range(half) if d < 0 else range(half, N_BUCKETS) if d > 0 else range(N_BUCKETS)):
            acc = jnp.where(bk == b, rb_ref[b, h] * LOG2E, acc)
        o_ref[i] = acc


def _bias_tiles(rel_bias, t):
    assert t >= REL_MAX_DIST
    nh = rel_bias.shape[1]
    i = jnp.arange(t, dtype=jnp.int32)
    rel0 = i[:, None] - i[None, :]
    nt = 2 * BIAS_REACH + 1
    buckets = jnp.stack([_t5_bucket(rel0 + d * t) for d in (-1, 0, 1)])
    return pl.pallas_call(
        _bias_kernel,
        out_shape=jax.ShapeDtypeStruct((nh, nt, t, t), F32),
        grid=(nh,),
        in_specs=[pl.BlockSpec(memory_space=pltpu.SMEM),
                  pl.BlockSpec((3, t, t), lambda h: (0, 0, 0))],
        out_specs=pl.BlockSpec((None, nt, t, t), lambda h: (h, 0, 0, 0)),
        compiler_params=_params(("parallel",)),
        name="bias_tiles",
    )(rel_bias, buckets)


def _attn_kernel(lamp_ref, q_ref, k_ref, v_ref, bias_ref, g_ref, o_ref,
                 vt_ref, qt_ref, s_ref, mt_ref, m_ref, acc_ref, *, t, seq_len, lam_init):
    nk = seq_len // t
    d = HEAD_DIM
    assert ATTN_KEYS_PER_TRIP % 2 == 0 and nk % ATTN_KEYS_PER_TRIP == 0

    row = lax.broadcasted_iota(jnp.int32, (d, t), 0)
    for c in range(nk):
        cols = slice(c * t, (c + 1) * t)
        vt_ref[:d, cols] = v_ref[cols, :].astype(F32).T.astype(BF16)
        qt = q_ref[cols, :].astype(F32).T
        qt_ref[0, :, cols] = jnp.where(row < DIFF_QK_DIM, qt, 0.0).astype(BF16)
        qt_ref[1, :, cols] = jnp.where(row >= DIFF_QK_DIM, qt, 0.0).astype(BF16)
    vt_ref[d:, :] = jnp.ones((SUM_ROWS, seq_len), BF16)

    lp = lamp_ref[...]
    lam = (jnp.exp(jnp.sum(lp[0:1] * lp[1:2], axis=1, keepdims=True))
           - jnp.exp(jnp.sum(lp[2:3] * lp[3:4], axis=1, keepdims=True)) + lam_init)

    def scores(qi, j, slot):
        kt = k_ref[pl.ds(pl.multiple_of(j * t, t), t), :]
        bias = bias_ref[jnp.clip(j - qi, -BIAS_REACH, BIAS_REACH) + BIAS_REACH]
        for idx in range(2):
            qt = qt_ref[idx, :, pl.ds(pl.multiple_of(qi * t, t), t)]
            s = jnp.dot(kt, qt, preferred_element_type=F32) + bias
            s_ref[slot, idx] = s
            mt_ref[slot, idx] = jnp.max(s, axis=0, keepdims=True)

    def consume(j, slot):
        vt = vt_ref[:, pl.ds(pl.multiple_of(j * t, t), t)]
        for idx in range(2):
            m_old = m_ref[idx]
            m_new = jnp.maximum(m_old, mt_ref[slot, idx])
            alpha = jnp.exp2(m_old - m_new)
            p = jnp.exp2(s_ref[slot, idx] - m_new)
            m_ref[idx] = m_new
            acc_ref[idx] = alpha * acc_ref[idx] + jnp.dot(vt, p.astype(BF16), preferred_element_type=F32)

    scores(0, 0, 0)

    def query_tile(qi, carry):
        m_ref[...] = jnp.full(m_ref.shape, -jnp.inf, F32)
        acc_ref[...] = jnp.zeros(acc_ref.shape, F32)

        def key_group(i, c):
            for u in range(ATTN_KEYS_PER_TRIP):
                j = ATTN_KEYS_PER_TRIP * i + u
                wrap = j + 1 == nk
                scores(jnp.where(wrap, jnp.minimum(qi + 1, nk - 1), qi), jnp.where(wrap, 0, j + 1), (u + 1) % 2)
                consume(j, u % 2)
            return c

        lax.fori_loop(0, nk // ATTN_KEYS_PER_TRIP, key_group, 0)

        out = acc_ref[0, :d] / acc_ref[0, d:d + 1] - lam * (acc_ref[1, :d] / acc_ref[1, d:d + 1])
        y = out * lax.rsqrt(jnp.mean(out * out, axis=0, keepdims=True) + EPS)
        o_ref[pl.ds(pl.multiple_of(qi * t, t), t), :] = (y.T * g_ref[...] * (1.0 - lam_init)).astype(BF16)
        return carry

    lax.fori_loop(0, nk, query_tile, 0)


def _diff_attention(proj, bias_tiles, lam_params, head_g, *, batch, seq_len, lam_init):
    n = proj.shape[1]
    nh = proj.shape[2] // HEAD_DIM
    t = ATTN_TILE
    nt = bias_tiles.shape[1]
    blk = lambda g: pl.BlockSpec((None, seq_len, HEAD_DIM), lambda h, b: (g, b, h))
    return pl.pallas_call(
        functools.partial(_attn_kernel, t=t, seq_len=seq_len, lam_init=lam_init),
        out_shape=jax.ShapeDtypeStruct((n, nh * HEAD_DIM), BF16),
        grid=(nh, batch),
        in_specs=[pl.BlockSpec(lam_params.shape, lambda h, b: (0, 0)),
                  blk(0), blk(1), blk(2),
                  pl.BlockSpec((None, nt, t, t), lambda h, b: (h, 0, 0, 0)),
                  pl.BlockSpec((1, HEAD_DIM), lambda h, b: (0, 0))],
        out_specs=pl.BlockSpec((seq_len, HEAD_DIM), lambda h, b: (b, h)),
        scratch_shapes=[pltpu.VMEM((HEAD_DIM + SUM_ROWS, seq_len), BF16),
                        pltpu.VMEM((2, HEAD_DIM, seq_len), BF16),
                        pltpu.VMEM((2, 2, t, t), F32), pltpu.VMEM((2, 2, 1, t), F32),
                        pltpu.VMEM((2, 1, t), F32),
                        pltpu.VMEM((2, HEAD_DIM + SUM_ROWS, t), F32)],
        compiler_params=_params(("parallel", "parallel")),
        name="diff_attention",
    )(lam_params, proj, proj, proj, bias_tiles, head_g)


def _log_sigmoid(x):
    return jnp.minimum(x, 0.0) - jnp.log1p(jnp.exp(-jnp.abs(x)))


def _ret_kernel(df_ref, db_ref, q_ref, k_ref, v_ref, rg_ref, g_ref, o_ref, u_ref, r_ref, *, seq_len, c):
    h = pl.program_id(1)
    n = seq_len // c
    d = HEAD_DIM
    df, db = df_ref[0, h], db_ref[0, h]

    lgf = _log_sigmoid(jnp.full((c, d), df, F32))
    lgb = _log_sigmoid(jnp.full((c, d), db, F32))
    ri = lax.broadcasted_iota(jnp.int32, (c, d), 0).astype(F32)
    xi_f, ze_f = jnp.exp(lgf * (ri + 1.0)), jnp.exp(lgf * (c - 1.0 - ri))
    xi_b, ze_b = jnp.exp(lgb * (c - ri)), jnp.exp(lgb * ri)
    gc_f = jnp.exp(_log_sigmoid(jnp.full((d, d), df, F32)) * c)
    gc_b = jnp.exp(_log_sigmoid(jnp.full((d, d), db, F32)) * c)

    diff = (lax.broadcasted_iota(jnp.int32, (c, c), 0) - lax.broadcasted_iota(jnp.int32, (c, c), 1)).astype(F32)
    lgf2 = _log_sigmoid(jnp.full((c, c), df, F32))
    lgb2 = _log_sigmoid(jnp.full((c, c), db, F32))
    decay = jnp.where(diff >= 0, jnp.exp(lgf2 * jnp.maximum(diff, 0.0)), jnp.exp(lgb2 * jnp.maximum(-diff, 0.0)))

    def chunk(ref, i):
        return ref[pl.ds(pl.multiple_of(i * c, c), c), :]

    def local_state(i, carry):
        kc = chunk(k_ref, i).astype(F32)
        kz = jnp.concatenate([kc * ze_f, kc * ze_b], axis=1).astype(BF16)
        u_ref[i] = lax.dot_general(kz, chunk(v_ref, i), (((0,), (0,)), ((), ())), preferred_element_type=F32)
        return carry

    lax.fori_loop(0, n, local_state, 0, unroll=RET_UNROLL)

    def scan_fwd(i, r):
        r_ref[i, :d, :] = r.astype(BF16)
        return gc_f * r + u_ref[i, :d, :]

    def scan_bwd(i, r):
        i = n - 1 - i
        r_ref[i, d:, :] = r.astype(BF16)
        return gc_b * r + u_ref[i, d:, :]

    lax.fori_loop(0, n, scan_fwd, jnp.zeros((d, d), F32))
    lax.fori_loop(0, n, scan_bwd, jnp.zeros((d, d), F32))

    def outputs(i, carry):
        qb = chunk(q_ref, i)
        qc = qb.astype(F32)
        vc = chunk(v_ref, i)
        qx = jnp.concatenate([qc * xi_f, qc * xi_b], axis=1).astype(BF16)
        inter = jnp.dot(qx, r_ref[i], preferred_element_type=F32)
        inner = lax.dot_general(qb, chunk(k_ref, i), (((1,), (1,)), ((), ())), preferred_element_type=F32) * decay
        o = inter + jnp.dot(inner.astype(BF16), vc, preferred_element_type=F32)
        rg = chunk(rg_ref, i).astype(F32)
        o_ref[pl.ds(pl.multiple_of(i * c, c), c), :] = (_rms_rows(o, g_ref[...]) * _silu(rg)).astype(BF16)
        return carry

    lax.fori_loop(0, n, outputs, 0, unroll=RET_UNROLL)


def _retention(proj, decay_f, decay_b, head_g, *, batch, seq_len):
    n = proj.shape[1]
    nh = proj.shape[2] // HEAD_DIM
    c = RET_CHUNK
    blk = lambda g: pl.BlockSpec((None, seq_len, HEAD_DIM), lambda b, h: (g, b, h))
    smem = pl.BlockSpec(memory_space=pltpu.SMEM)
    return pl.pallas_call(
        functools.partial(_ret_kernel, seq_len=seq_len, c=c),
        out_shape=jax.ShapeDtypeStruct((n, nh * HEAD_DIM), BF16),
        grid=(batch, nh),
        in_specs=[smem, smem, blk(3), blk(4), blk(5), blk(6),
                  pl.BlockSpec((1, HEAD_DIM), lambda b, h: (0, 0))],
        out_specs=pl.BlockSpec((seq_len, HEAD_DIM), lambda b, h: (b, h)),
        scratch_shapes=[pltpu.VMEM((seq_len // c, 2 * HEAD_DIM, HEAD_DIM), F32),
                        pltpu.VMEM((seq_len // c, 2 * HEAD_DIM, HEAD_DIM), BF16)],
        compiler_params=_params(("parallel", "parallel")),
        name="retention",
    )(decay_f, decay_b, proj, proj, proj, proj, head_g)


def _outproj_kernel(x_ref, gt_ref, a_ref, r_ref, wa_ref, wr_ref, o_ref):
    y = (jnp.dot(a_ref[...], wa_ref[...], preferred_element_type=F32)
         + jnp.dot(r_ref[...], wr_ref[...], preferred_element_type=F32))
    o_ref[...] = x_ref[...] + gt_ref[...] * y


def _outproj(x, gate, d_out, r_out, w_out, *, seq_len):
    n, d = x.shape
    kw = d_out.shape[1]
    tm = PROJ_ROW_TILE
    row = lambda i: (i, 0)
    return pl.pallas_call(
        _outproj_kernel,
        out_shape=jax.ShapeDtypeStruct((n, d), F32),
        grid=(n // tm,),
        in_specs=[pl.BlockSpec((tm, d), row),
                  pl.BlockSpec((None, 1, d), lambda i: ((i * tm) // seq_len, 0, 0)),
                  pl.BlockSpec((tm, kw), row),
                  pl.BlockSpec((tm, kw), row),
                  pl.BlockSpec((kw, d), lambda i: (0, 0)),
                  pl.BlockSpec((kw, d), lambda i: (1, 0))],
        out_specs=pl.BlockSpec((tm, d), row),
        compiler_params=_params(("parallel",)),
        name="outproj",
    )(x, gate, d_out, r_out, w_out, w_out)


def _rotary_tables(seq_len):
    pos = jnp.arange(seq_len, dtype=F32)
    inv = ROPE_BASE ** (-jnp.arange(0, HEAD_DIM, 2, dtype=F32) / HEAD_DIM)
    ang = pos[:, None] * inv[None, :]
    cos, sin = jnp.cos(ang), jnp.sin(ang)
    return jnp.concatenate([cos, cos], axis=1), jnp.concatenate([-sin, sin], axis=1)


def _trunk(x, mod, w, bias_tiles):
    batch, seq_len, d = x.shape
    x = x.reshape(batch * seq_len, d)
    sh1, sc1, g1, shm, scm, gm, sh2, sc2, g2 = [mod[:, i].reshape(batch, 1, d) for i in range(N_MOD)]
    lam_init = 0.8 - 0.6 * math.exp(-0.3 * 0)
    cos_t, sin_t = _rotary_tables(seq_len)

    x = _ffn(x, sh1, sc1, g1, w["ffn1_norm_g"], w["final_norm_g"], w["ffn1_w13"], w["ffn1_w2"],
             seq_len=seq_len, final_norm=False)
    proj = _inproj(x, shm, scm, w["mix_norm_g"], cos_t, sin_t, w["w_in"], seq_len=seq_len)
    d_out = _diff_attention(proj, bias_tiles, w["lam_params"], w["diff_head_g"],
                            batch=batch, seq_len=seq_len, lam_init=lam_init)
    r_out = _retention(proj, w["ret_decay_fwd"], w["ret_decay_bwd"], w["ret_head_g"],
                       batch=batch, seq_len=seq_len)
    x = _outproj(x, gm, d_out, r_out, w["w_out"], seq_len=seq_len)
    x = _ffn(x, sh2, sc2, g2, w["ffn2_norm_g"], w["final_norm_g"], w["ffn2_w13"], w["ffn2_w2"],
             seq_len=seq_len, final_norm=True)
    return x.reshape(batch, seq_len, d)


def kernel(x_prompt, x_sample, c_prompt, c_sample, ada_w, ada_b, ffn1_norm_g, ffn1_w13, ffn1_w2, mix_norm_g, w_in, diff_lambda_q1, diff_lambda_k1, diff_lambda_q2, diff_lambda_k2, diff_head_g, rel_bias, ret_decay_fwd, ret_decay_bwd, ret_head_g, w_out, ffn2_norm_g, ffn2_w13, ffn2_w2, final_norm_g):
    assert ada_w.shape[0] == 1, "single-layer trunk"
    d = x_prompt.shape[-1]
    nb = c_prompt.shape[0]
    row = lambda a: a.reshape(1, -1)
    w = {
        "ffn1_norm_g": ffn1_norm_g, "mix_norm_g": mix_norm_g, "ffn2_norm_g": ffn2_norm_g,
        "final_norm_g": row(final_norm_g), "diff_head_g": diff_head_g, "ret_head_g": ret_head_g,
        "ffn1_w13": ffn1_w13[0].astype(BF16), "ffn1_w2": ffn1_w2[0].astype(BF16),
        "ffn2_w13": ffn2_w13[0].astype(BF16), "ffn2_w2": ffn2_w2[0].astype(BF16),
        "w_in": w_in[0].astype(BF16), "w_out": w_out[0].astype(BF16),
        "ret_decay_fwd": ret_decay_fwd, "ret_decay_bwd": ret_decay_bwd,
        "lam_params": jnp.concatenate([diff_lambda_q1, diff_lambda_k1, diff_lambda_q2, diff_lambda_k2], axis=0),
    }
    mod = _ada_mod(jnp.concatenate([c_prompt, c_sample], axis=0), ada_w[0], ada_b[0])
    mod = mod.reshape(mod.shape[0], N_MOD, d)
    bias_tiles = _bias_tiles(rel_bias, ATTN_TILE)
    y_prompt = _trunk(x_prompt, mod[:nb], w, bias_tiles)
    y_sample = _trunk(x_sample, mod[nb:], w, bias_tiles)
    return (y_prompt, y_sample)
```

```python
import functools
import math

import jax
import jax.numpy as jnp
from jax import lax
from jax.experimental import pallas as pl
from jax.experimental.pallas import tpu as pltpu

F32 = jnp.float32
BF16 = jnp.bfloat16

HEAD_DIM = 128
DIFF_QK_DIM = HEAD_DIM // 2
N_BUCKETS = 32
REL_MAX_DIST = 128
ROPE_BASE = 10000.0
N_MOD = 9
EPS = 1e-6
LOG2E = math.log2(math.e)
N_PROJ = 7

V7X_VMEM_BYTES = 64 * 1024 * 1024
VMEM_LIMIT = 56 * 1024 * 1024

FFN_ROW_TILE = 1024
FFN_HID_TILE = 512
INPROJ_ROW_TILE = 256
PROJ_ROW_TILE = 512
NORM_ROW_BLOCK = 64
NORM_UNROLL = 2
ATTN_TILE = 512
ATTN_KEYS_PER_TRIP = 8
ATTN_BOUND_SLACK = 1.0
ATTN_MIN_DENOM = 2.0 ** -60
SUM_ROWS = 16
BIAS_REACH = 2
RET_CHUNK = 256
RET_UNROLL = 8
ADA_COL_TILE = 1024


def _params(sem):
    return pltpu.CompilerParams(dimension_semantics=sem, vmem_limit_bytes=VMEM_LIMIT)


def _rms_rows(x, g):
    return x * lax.rsqrt(jnp.mean(x * x, axis=-1, keepdims=True) + EPS) * g


def _silu(x):
    return x * jax.nn.sigmoid(x)


def _norm_modulate(x_ref, ng_ref, sc_ref, sh_ref, h_ref):
    gain = ng_ref[...] * (1.0 + sc_ref[...])
    shift = sh_ref[...]

    def block(i, carry):
        rows = pl.ds(pl.multiple_of(i * NORM_ROW_BLOCK, NORM_ROW_BLOCK), NORM_ROW_BLOCK)
        x = x_ref[rows, :]
        r = lax.rsqrt(jnp.mean(x * x, axis=-1, keepdims=True) + EPS)
        h_ref[rows, :] = (x * r * gain + shift).astype(BF16)
        return carry

    lax.fori_loop(0, x_ref.shape[0] // NORM_ROW_BLOCK, block, 0, unroll=NORM_UNROLL)


def _ada_kernel(c_ref, w_ref, b_ref, o_ref):
    a = _silu(c_ref[...]).astype(BF16)
    o_ref[...] = jnp.dot(a, w_ref[...].astype(BF16), preferred_element_type=F32) + b_ref[...]


def _ada_mod(c, w, b):
    nb, d = c.shape
    n = w.shape[1]
    tn = ADA_COL_TILE
    return pl.pallas_call(
        _ada_kernel,
        out_shape=jax.ShapeDtypeStruct((nb, n), F32),
        grid=(n // tn,),
        in_specs=[pl.BlockSpec((nb, d), lambda j: (0, 0)),
                  pl.BlockSpec((d, tn), lambda j: (0, j)),
                  pl.BlockSpec((1, tn), lambda j: (0, j))],
        out_specs=pl.BlockSpec((nb, tn), lambda j: (0, j)),
        compiler_params=_params(("parallel",)),
        name="ada_mod",
    )(c, w, b.reshape(1, n))


def _ffn_kernel(x_ref, sh_ref, sc_ref, gt_ref, ng_ref, fg_ref, w1_ref, w3_ref, w2_ref,
                o_ref, h_ref, *, final_norm):
    j = pl.program_id(1)

    @pl.when(j == 0)
    def _():
        _norm_modulate(x_ref, ng_ref, sc_ref, sh_ref, h_ref)
        o_ref[...] = jnp.zeros(o_ref.shape, F32)

    h = h_ref[...]
    g = jnp.dot(h, w1_ref[...], preferred_element_type=F32)
    u = jnp.dot(h, w3_ref[...], preferred_element_type=F32)
    o_ref[...] += jnp.dot((_silu(g) * u).astype(BF16), w2_ref[...], preferred_element_type=F32)

    @pl.when(j == pl.num_programs(1) - 1)
    def _():
        y = x_ref[...] + 0.5 * gt_ref[...] * o_ref[...]
        if final_norm:
            y = _rms_rows(y, fg_ref[...])
        o_ref[...] = y


def _ffn(x, shift, scale, gate, norm_g, final_g, w13, w2, *, seq_len, final_norm):
    n, d = x.shape
    tm, tf = FFN_ROW_TILE, FFN_HID_TILE
    nf = w2.shape[0] // tf
    seq = lambda i, j: ((i * tm) // seq_len, 0, 0)
    row = lambda i, j: (i, 0)
    const = lambda i, j: (0, 0)
    return pl.pallas_call(
        functools.partial(_ffn_kernel, final_norm=final_norm),
        out_shape=jax.ShapeDtypeStruct((n, d), F32),
        grid=(n // tm, nf),
        in_specs=[pl.BlockSpec((tm, d), row, pipeline_mode=pl.Buffered(1)),
                  pl.BlockSpec((None, 1, d), seq),
                  pl.BlockSpec((None, 1, d), seq),
                  pl.BlockSpec((None, 1, d), seq),
                  pl.BlockSpec((1, d), const),
                  pl.BlockSpec((1, d), const),
                  pl.BlockSpec((d, tf), lambda i, j: (0, j)),
                  pl.BlockSpec((d, tf), lambda i, j: (0, nf + j)),
                  pl.BlockSpec((tf, d), lambda i, j: (j, 0))],
        out_specs=pl.BlockSpec((tm, d), row),
        scratch_shapes=[pltpu.VMEM((tm, d), BF16)],
        compiler_params=_params(("parallel", "arbitrary")),
        name="ffn_final" if final_norm else "ffn",
    )(x, shift, scale, gate, norm_g, final_g, w13, w13, w2)


def _inproj_kernel(x_ref, sh_ref, sc_ref, ng_ref, cq_ref, sq_ref, w_ref, o_ref, h_ref):
    _norm_modulate(x_ref, ng_ref, sc_ref, sh_ref, h_ref)
    h = h_ref[...]
    cos, sin = cq_ref[...], sq_ref[...]
    gw = w_ref.shape[1] // N_PROJ
    for j in range(N_PROJ):
        res = jnp.dot(h, w_ref[:, j * gw:(j + 1) * gw], preferred_element_type=F32)
        if j == 0:
            o_ref[j] = (res * (DIFF_QK_DIM ** -0.5 * LOG2E)).astype(BF16)
        elif j in (3, 4):
            scale = 1.0 if j == 3 else HEAD_DIM ** -0.5
            for hd in range(res.shape[1] // HEAD_DIM):
                xs = res[:, hd * HEAD_DIM:(hd + 1) * HEAD_DIM]
                rot = pltpu.roll(xs, HEAD_DIM // 2, 1)
                o_ref[j, :, hd * HEAD_DIM:(hd + 1) * HEAD_DIM] = ((xs * cos + rot * sin) * scale).astype(BF16)
        else:
            o_ref[j] = res.astype(BF16)


def _inproj(x, shift, scale, norm_g, cos_t, sin_t, w_in, *, seq_len):
    n, d = x.shape
    gw = w_in.shape[1] // N_PROJ
    tm = INPROJ_ROW_TILE
    spt = seq_len // tm
    seq = lambda i: ((i * tm) // seq_len, 0, 0)
    return pl.pallas_call(
        _inproj_kernel,
        out_shape=jax.ShapeDtypeStruct((N_PROJ, n, gw), BF16),
        grid=(n // tm,),
        in_specs=[pl.BlockSpec((tm, d), lambda i: (i, 0)),
                  pl.BlockSpec((None, 1, d), seq),
                  pl.BlockSpec((None, 1, d), seq),
                  pl.BlockSpec((1, d), lambda i: (0, 0)),
                  pl.BlockSpec((tm, HEAD_DIM), lambda i: (i % spt, 0)),
                  pl.BlockSpec((tm, HEAD_DIM), lambda i: (i % spt, 0)),
                  pl.BlockSpec(w_in.shape, lambda i: (0, 0), pipeline_mode=pl.Buffered(1))],
        out_specs=pl.BlockSpec((N_PROJ, tm, gw), lambda i: (0, i, 0)),
        scratch_shapes=[pltpu.VMEM((tm, d), BF16)],
        compiler_params=_params(("parallel",)),
        name="inproj",
    )(x, shift, scale, norm_g, cos_t, sin_t, w_in)


def _t5_bucket(rel):
    half = N_BUCKETS // 2
    max_exact = half // 2
    ret = jnp.where(rel > 0, half, 0)
    n = jnp.abs(rel)
    nf = jnp.maximum(n, 1).astype(F32)
    large = max_exact + (jnp.log(nf / max_exact) / math.log(REL_MAX_DIST / max_exact)
                         * (half - max_exact)).astype(jnp.int32)
    large = jnp.minimum(large, half - 1)
    return ret + jnp.where(n < max_exact, n, large)


def _bias_kernel(rb_ref, bk_ref, o_ref):
    h = pl.program_id(0)
    half = N_BUCKETS // 2
    for i in range(o_ref.shape[0]):
        d = i - BIAS_REACH
        if abs(d) > 1:
            b = half - 1 if d < 0 else N_BUCKETS - 1
            o_ref[i] = jnp.full(o_ref.shape[1:], rb_ref[b, h] * LOG2E, F32)
            continue
        bk = bk_ref[d + 1]
        acc = jnp.zeros(bk.shape, F32)
        for b in (range(half) if d < 0 else range(half, N_BUCKETS) if d > 0 else range(N_BUCKETS)):
            acc = jnp.where(bk == b, rb_ref[b, h] * LOG2E, acc)
        o_ref[i] = acc


def _bias_tiles(rel_bias, t):
    assert t >= REL_MAX_DIST
    nh = rel_bias.shape[1]
    i = jnp.arange(t, dtype=jnp.int32)
    rel0 = i[:, None] - i[None, :]
    nt = 2 * BIAS_REACH + 1
    buckets = jnp.stack([_t5_bucket(rel0 + d * t) for d in (-1, 0, 1)])
    return pl.pallas_call(
        _bias_kernel,
        out_shape=jax.ShapeDtypeStruct((nh, nt, t, t), F32),
        grid=(nh,),
        in_specs=[pl.BlockSpec(memory_space=pltpu.SMEM),
                  pl.BlockSpec((3, t, t), lambda h: (0, 0, 0))],
        out_specs=pl.BlockSpec((None, nt, t, t), lambda h: (h, 0, 0, 0)),
        compiler_params=_params(("parallel",)),
        name="bias_tiles",
    )(rel_bias, buckets)


def _attn_kernel(lamp_ref, q_ref, k_ref, v_ref, bias_ref, g_ref, o_ref,
                 vt_ref, qt_ref, u_ref, m_ref, acc_ref, *, t, seq_len, lam_init):
    nk = seq_len // t
    d = HEAD_DIM
    kpt = min(ATTN_KEYS_PER_TRIP, nk)
    assert nk % kpt == 0

    row = lax.broadcasted_iota(jnp.int32, (d, t), 0)
    ksq = jnp.zeros((1, 1), F32)
    for c in range(nk):
        cols = slice(c * t, (c + 1) * t)
        vt_ref[:d, cols] = v_ref[cols, :].astype(F32).T.astype(BF16)
        qt = q_ref[cols, :].astype(F32).T
        for idx, qmask in enumerate((row < DIFF_QK_DIM, row >= DIFF_QK_DIM)):
            qh = jnp.where(qmask, qt, 0.0)
            qt_ref[idx, :, cols] = qh.astype(BF16)
            u_ref[idx, :, cols] = jnp.sum(qh * qh, axis=0, keepdims=True)
        kn = jnp.sum(jnp.square(k_ref[cols, :].astype(F32)), axis=1, keepdims=True)
        ksq = jnp.maximum(ksq, jnp.max(kn, axis=0, keepdims=True))
    vt_ref[d:, :] = jnp.ones((SUM_ROWS, seq_len), BF16)
    center = bias_ref[BIAS_REACH]
    bmax = jnp.max(jnp.max(center, axis=0, keepdims=True), axis=1, keepdims=True)
    u_ref[...] = jnp.sqrt(u_ref[...] * ksq) + (bmax + ATTN_BOUND_SLACK)

    lp = lamp_ref[...]
    lam = (jnp.exp(jnp.sum(lp[0:1] * lp[1:2], axis=1, keepdims=True))
           - jnp.exp(jnp.sum(lp[2:3] * lp[3:4], axis=1, keepdims=True)) + lam_init)

    def tile_operands(qi, j):
        kt = k_ref[pl.ds(pl.multiple_of(j * t, t), t), :]
        bias = bias_ref[jnp.clip(j - qi, -BIAS_REACH, BIAS_REACH) + BIAS_REACH]
        return kt, bias

    def queries(ref, idx, qi):
        return ref[idx, :, pl.ds(pl.multiple_of(qi * t, t), t)]

    def values(j):
        return vt_ref[:, pl.ds(pl.multiple_of(j * t, t), t)]

    def bounded_tile(qi):
        acc_ref[...] = jnp.zeros(acc_ref.shape, F32)

        def accumulate(j, ps):
            vt = values(j)
            for idx in range(2):
                acc_ref[idx] += jnp.dot(vt, ps[idx], preferred_element_type=F32)

        def key_group(i, c):
            pending = None
            for u in range(kpt):
                j = kpt * i + u
                kt, bias = tile_operands(qi, j)
                ps = []
                for idx in range(2):
                    s = jnp.dot(kt, queries(qt_ref, idx, qi), preferred_element_type=F32)
                    ps.append(jnp.exp2(s + bias - queries(u_ref, idx, qi)).astype(BF16))
                if pending is not None:
                    accumulate(*pending)
                pending = (j, ps)
            accumulate(*pending)
            return c

        lax.fori_loop(0, nk // kpt, key_group, 0)

    def running_max_tile(qi):
        m_ref[...] = jnp.full(m_ref.shape, -jnp.inf, F32)
        acc_ref[...] = jnp.zeros(acc_ref.shape, F32)

        def key_tile(j, c):
            kt, bias = tile_operands(qi, j)
            vt = values(j)
            for idx in range(2):
                s = jnp.dot(kt, queries(qt_ref, idx, qi), preferred_element_type=F32) + bias
                m_old = m_ref[idx]
                m_new = jnp.maximum(m_old, jnp.max(s, axis=0, keepdims=True))
                p = jnp.exp2(s - m_new).astype(BF16)
                m_ref[idx] = m_new
                acc_ref[idx] = jnp.exp2(m_old - m_new) * acc_ref[idx] + jnp.dot(vt, p, preferred_element_type=F32)
            return c

        lax.fori_loop(0, nk, key_tile, 0)

    def query_tile(qi, carry):
        bounded_tile(qi)
        denom = jnp.minimum(acc_ref[0, d:d + 1], acc_ref[1, d:d + 1])

        @pl.when(jnp.logical_not(jnp.min(denom) >= ATTN_MIN_DENOM))
        def _():
            running_max_tile(qi)

        out = acc_ref[0, :d] / acc_ref[0, d:d + 1] - lam * (acc_ref[1, :d] / acc_ref[1, d:d + 1])
        y = out * lax.rsqrt(jnp.mean(out * out, axis=0, keepdims=True) + EPS)
        o_ref[pl.ds(pl.multiple_of(qi * t, t), t), :] = (y.T * g_ref[...] * (1.0 - lam_init)).astype(BF16)
        return carry

    lax.fori_loop(0, nk, query_tile, 0)


def _diff_attention(proj, bias_tiles, lam_params, head_g, *, batch, seq_len, lam_init):
    n = proj.shape[1]
    nh = proj.shape[2] // HEAD_DIM
    t = ATTN_TILE
    nt = bias_tiles.shape[1]
    blk = lambda g: pl.BlockSpec((None, seq_len, HEAD_DIM), lambda h, b: (g, b, h))
    return pl.pallas_call(
        functools.partial(_attn_kernel, t=t, seq_len=seq_len, lam_init=lam_init),
        out_shape=jax.ShapeDtypeStruct((n, nh * HEAD_DIM), BF16),
        grid=(nh, batch),
        in_specs=[pl.BlockSpec(lam_params.shape, lambda h, b: (0, 0)),
                  blk(0), blk(1), blk(2),
                  pl.BlockSpec((None, nt, t, t), lambda h, b: (h, 0, 0, 0)),
                  pl.BlockSpec((1, HEAD_DIM), lambda h, b: (0, 0))],
        out_specs=pl.BlockSpec((seq_len, HEAD_DIM), lambda h, b: (b, h)),
        scratch_shapes=[pltpu.VMEM((HEAD_DIM + SUM_ROWS, seq_len), BF16),
                        pltpu.VMEM((2, HEAD_DIM, seq_len), BF16),
                        pltpu.VMEM((2, 1, seq_len), F32),
                        pltpu.VMEM((2, 1, t), F32),
                        pltpu.VMEM((2, HEAD_DIM + SUM_ROWS, t), F32)],
        compiler_params=_params(("parallel", "parallel")),
        name="diff_attention",
    )(lam_params, proj, proj, proj, bias_tiles, head_g)


def _log_sigmoid(x):
    return jnp.minimum(x, 0.0) - jnp.log1p(jnp.exp(-jnp.abs(x)))


def _ret_kernel(df_ref, db_ref, q_ref, k_ref, v_ref, rg_ref, g_ref, o_ref, u_ref, r_ref, *, seq_len, c):
    h = pl.program_id(1)
    n = seq_len // c
    d = HEAD_DIM
    df, db = df_ref[0, h], db_ref[0, h]

    lgf = _log_sigmoid(jnp.full((c, d), df, F32))
    lgb = _log_sigmoid(jnp.full((c, d), db, F32))
    ri = lax.broadcasted_iota(jnp.int32, (c, d), 0).astype(F32)
    xi_f, ze_f = jnp.exp(lgf * (ri + 1.0)), jnp.exp(lgf * (c - 1.0 - ri))
    xi_b, ze_b = jnp.exp(lgb * (c - ri)), jnp.exp(lgb * ri)
    gc_f = jnp.exp(_log_sigmoid(jnp.full((d, d), df, F32)) * c)
    gc_b = jnp.exp(_log_sigmoid(jnp.full((d, d), db, F32)) * c)

    diff = (lax.broadcasted_iota(jnp.int32, (c, c), 0) - lax.broadcasted_iota(jnp.int32, (c, c), 1)).astype(F32)
    lgf2 = _log_sigmoid(jnp.full((c, c), df, F32))
    lgb2 = _log_sigmoid(jnp.full((c, c), db, F32))
    decay = jnp.where(diff >= 0, jnp.exp(lgf2 * jnp.maximum(diff, 0.0)), jnp.exp(lgb2 * jnp.maximum(-diff, 0.0)))

    def chunk(ref, i):
        return ref[pl.ds(pl.multiple_of(i * c, c), c), :]

    def local_state(i, carry):
        kc = chunk(k_ref, i).astype(F32)
        kz = jnp.concatenate([kc * ze_f, kc * ze_b], axis=1).astype(BF16)
        u_ref[i] = lax.dot_general(kz, chunk(v_ref, i), (((0,), (0,)), ((), ())), preferred_element_type=F32)
        return carry

    lax.fori_loop(0, n, local_state, 0, unroll=RET_UNROLL)

    def scan_fwd(i, r):
        r_ref[i, :d, :] = r.astype(BF16)
        return gc_f * r + u_ref[i, :d, :]

    def scan_bwd(i, r):
        i = n - 1 - i
        r_ref[i, d:, :] = r.astype(BF16)
        return gc_b * r + u_ref[i, d:, :]

    lax.fori_loop(0, n, scan_fwd, jnp.zeros((d, d), F32))
    lax.fori_loop(0, n, scan_bwd, jnp.zeros((d, d), F32))

    def outputs(i, carry):
        qb = chunk(q_ref, i)
        qc = qb.astype(F32)
        vc = chunk(v_ref, i)
        qx = jnp.concatenate([qc * xi_f, qc * xi_b], axis=1).astype(BF16)
        inter = jnp.dot(qx, r_ref[i], preferred_element_type=F32)
        inner = lax.dot_general(qb, chunk(k_ref, i), (((1,), (1,)), ((), ())), preferred_element_type=F32) * decay
        o = inter + jnp.dot(inner.astype(BF16), vc, preferred_element_type=F32)
        rg = chunk(rg_ref, i).astype(F32)
        o_ref[pl.ds(pl.multiple_of(i * c, c), c), :] = (_rms_rows(o, g_ref[...]) * _silu(rg)).astype(BF16)
        return carry

    lax.fori_loop(0, n, outputs, 0, unroll=RET_UNROLL)


def _retention(proj, decay_f, decay_b, head_g, *, batch, seq_len):
    n = proj.shape[1]
    nh = proj.shape[2] // HEAD_DIM
    c = RET_CHUNK
    blk = lambda g: pl.BlockSpec((None, seq_len, HEAD_DIM), lambda b, h: (g, b, h))
    smem = pl.BlockSpec(memory_space=pltpu.SMEM)
    return pl.pallas_call(
        functools.partial(_ret_kernel, seq_len=seq_len, c=c),
        out_shape=jax.ShapeDtypeStruct((n, nh * HEAD_DIM), BF16),
        grid=(batch, nh),
        in_specs=[smem, smem, blk(3), blk(4), blk(5), blk(6),
                  pl.BlockSpec((1, HEAD_DIM), lambda b, h: (0, 0))],
        out_specs=pl.BlockSpec((seq_len, HEAD_DIM), lambda b, h: (b, h)),
        scratch_shapes=[pltpu.VMEM((seq_len // c, 2 * HEAD_DIM, HEAD_DIM), F32),
                        pltpu.VMEM((seq_len // c, 2 * HEAD_DIM, HEAD_DIM), BF16)],
        compiler_params=_params(("parallel", "parallel")),
        name="retention",
    )(decay_f, decay_b, proj, proj, proj, proj, head_g)


def _outproj_kernel(x_ref, gt_ref, a_ref, r_ref, wa_ref, wr_ref, o_ref):
    y = (jnp.dot(a_ref[...], wa_ref[...], preferred_element_type=F32)
         + jnp.dot(r_ref[...], wr_ref[...], preferred_element_type=F32))
    o_ref[...] = x_ref[...] + gt_ref[...] * y


def _outproj(x, gate, d_out, r_out, w_out, *, seq_len):
    n, d = x.shape
    kw = d_out.shape[1]
    tm = PROJ_ROW_TILE
    row = lambda i: (i, 0)
    return pl.pallas_call(
        _outproj_kernel,
        out_shape=jax.ShapeDtypeStruct((n, d), F32),
        grid=(n // tm,),
        in_specs=[pl.BlockSpec((tm, d), row),
                  pl.BlockSpec((None, 1, d), lambda i: ((i * tm) // seq_len, 0, 0)),
                  pl.BlockSpec((tm, kw), row),
                  pl.BlockSpec((tm, kw), row),
                  pl.BlockSpec((kw, d), lambda i: (0, 0)),
                  pl.BlockSpec((kw, d), lambda i: (1, 0))],
        out_specs=pl.BlockSpec((tm, d), row),
        compiler_params=_params(("parallel",)),
        name="outproj",
    )(x, gate, d_out, r_out, w_out, w_out)


def _rotary_tables(seq_len):
    pos = jnp.arange(seq_len, dtype=F32)
    inv = ROPE_BASE ** (-jnp.arange(0, HEAD_DIM, 2, dtype=F32) / HEAD_DIM)
    ang = pos[:, None] * inv[None, :]
    cos, sin = jnp.cos(ang), jnp.sin(ang)
    return jnp.concatenate([cos, cos], axis=1), jnp.concatenate([-sin, sin], axis=1)


def _trunk(x, mod, w, bias_tiles):
    batch, seq_len, d = x.shape
    x = x.reshape(batch * seq_len, d)
    sh1, sc1, g1, shm, scm, gm, sh2, sc2, g2 = [mod[:, i].reshape(batch, 1, d) for i in range(N_MOD)]
    lam_init = 0.8 - 0.6 * math.exp(-0.3 * 0)
    cos_t, sin_t = _rotary_tables(seq_len)

    x = _ffn(x, sh1, sc1, g1, w["ffn1_norm_g"], w["final_norm_g"], w["ffn1_w13"], w["ffn1_w2"],
             seq_len=seq_len, final_norm=False)
    proj = _inproj(x, shm, scm, w["mix_norm_g"], cos_t, sin_t, w["w_in"], seq_len=seq_len)
    d_out = _diff_attention(proj, bias_tiles, w["lam_params"], w["diff_head_g"],
                            batch=batch, seq_len=seq_len, lam_init=lam_init)
    r_out = _retention(proj, w["ret_decay_fwd"], w["ret_decay_bwd"], w["ret_head_g"],
                       batch=batch, seq_len=seq_len)
    x = _outproj(x, gm, d_out, r_out, w["w_out"], seq_len=seq_len)
    x = _ffn(x, sh2, sc2, g2, w["ffn2_norm_g"], w["final_norm_g"], w["ffn2_w13"], w["ffn2_w2"],
             seq_len=seq_len, final_norm=True)
    return x.reshape(batch, seq_len, d)


def kernel(x_prompt, x_sample, c_prompt, c_sample, ada_w, ada_b, ffn1_norm_g, ffn1_w13, ffn1_w2, mix_norm_g, w_in, diff_lambda_q1, diff_lambda_k1, diff_lambda_q2, diff_lambda_k2, diff_head_g, rel_bias, ret_decay_fwd, ret_decay_bwd, ret_head_g, w_out, ffn2_norm_g, ffn2_w13, ffn2_w2, final_norm_g):
    assert ada_w.shape[0] == 1, "single-layer trunk"
    d = x_prompt.shape[-1]
    nb = c_prompt.shape[0]
    row = lambda a: a.reshape(1, -1)
    w = {
        "ffn1_norm_g": ffn1_norm_g, "mix_norm_g": mix_norm_g, "ffn2_norm_g": ffn2_norm_g,
        "final_norm_g": row(final_norm_g), "diff_head_g": diff_head_g, "ret_head_g": ret_head_g,
        "ffn1_w13": ffn1_w13[0].astype(BF16), "ffn1_w2": ffn1_w2[0].astype(BF16),
        "ffn2_w13": ffn2_w13[0].astype(BF16), "ffn2_w2": ffn2_w2[0].astype(BF16),
        "w_in": w_in[0].astype(BF16), "w_out": w_out[0].astype(BF16),
        "ret_decay_fwd": ret_decay_fwd, "ret_decay_bwd": ret_decay_bwd,
        "lam_params": jnp.concatenate([diff_lambda_q1, diff_lambda_k1, diff_lambda_q2, diff_lambda_k2], axis=0),
    }
    mod = _ada_mod(jnp.concatenate([c_prompt, c_sample], axis=0), ada_w[0], ada_b[0])
    mod = mod.reshape(mod.shape[0], N_MOD, d)
    bias_tiles = _bias_tiles(rel_bias, ATTN_TILE)
    y_prompt = _trunk(x_prompt, mod[:nb], w, bias_tiles)
    y_sample = _trunk(x_sample, mod[nb:], w, bias_tiles)
    return (y_prompt, y_sample)
```

```python
import functools
import math

import jax
import jax.numpy as jnp
from jax import lax
from jax.experimental import pallas as pl
from jax.experimental.pallas import tpu as pltpu

F32 = jnp.float32
BF16 = jnp.bfloat16

HEAD_DIM = 128
DIFF_QK_DIM = HEAD_DIM // 2
N_BUCKETS = 32
REL_MAX_DIST = 128
ROPE_BASE = 10000.0
N_MOD = 9
EPS = 1e-6
LOG2E = math.log2(math.e)
N_PROJ = 7

V7X_VMEM_BYTES = 64 * 1024 * 1024
VMEM_LIMIT = 56 * 1024 * 1024

FFN_ROW_TILE = 512
FFN_HID_TILE = 512
INPROJ_ROW_TILE = 256
PROJ_ROW_TILE = 512
NORM_ROW_BLOCK = 64
NORM_UNROLL = 2
ATTN_TILE = 1024
BIAS_TILE = 512
ATTN_MAX_KEY_TILES = 8
ATTN_BOUND_SLACK = 1.0
ATTN_MIN_DENOM = 2.0 ** -60
SUM_ROWS = 16
BIAS_REACH = 2
RET_CHUNK = 256
RET_UNROLL = 8
ADA_COL_TILE = 1024


def _params(sem):
    return pltpu.CompilerParams(dimension_semantics=sem, vmem_limit_bytes=VMEM_LIMIT)


def _rms_rows(x, g):
    return x * lax.rsqrt(jnp.mean(x * x, axis=-1, keepdims=True) + EPS) * g


def _silu(x):
    return x * jax.nn.sigmoid(x)


def _norm_modulate(x_ref, ng_ref, sc_ref, sh_ref, h_ref):
    gain = ng_ref[...] * (1.0 + sc_ref[...])
    shift = sh_ref[...]

    def block(i, carry):
        rows = pl.ds(pl.multiple_of(i * NORM_ROW_BLOCK, NORM_ROW_BLOCK), NORM_ROW_BLOCK)
        x = x_ref[rows, :]
        r = lax.rsqrt(jnp.mean(x * x, axis=-1, keepdims=True) + EPS)
        h_ref[rows, :] = (x * r * gain + shift).astype(BF16)
        return carry

    lax.fori_loop(0, x_ref.shape[0] // NORM_ROW_BLOCK, block, 0, unroll=NORM_UNROLL)


def _ada_kernel(c_ref, w_ref, b_ref, o_ref):
    a = _silu(c_ref[...]).astype(BF16)
    o_ref[...] = jnp.dot(a, w_ref[...].astype(BF16), preferred_element_type=F32) + b_ref[...]


def _ada_mod(c, w, b):
    nb, d = c.shape
    n = w.shape[1]
    tn = ADA_COL_TILE
    return pl.pallas_call(
        _ada_kernel,
        out_shape=jax.ShapeDtypeStruct((nb, n), F32),
        grid=(n // tn,),
        in_specs=[pl.BlockSpec((nb, d), lambda j: (0, 0)),
                  pl.BlockSpec((d, tn), lambda j: (0, j)),
                  pl.BlockSpec((1, tn), lambda j: (0, j))],
        out_specs=pl.BlockSpec((nb, tn), lambda j: (0, j)),
        compiler_params=_params(("parallel",)),
        name="ada_mod",
    )(c, w, b.reshape(1, n))


def _ffn_kernel(x_ref, sh_ref, sc_ref, gt_ref, ng_ref, fg_ref, w1_ref, w3_ref, w2_ref,
                o_ref, h_ref, *, final_norm):
    j = pl.program_id(1)

    @pl.when(j == 0)
    def _():
        _norm_modulate(x_ref, ng_ref, sc_ref, sh_ref, h_ref)
        o_ref[...] = jnp.zeros(o_ref.shape, F32)

    h = h_ref[...]
    g = jnp.dot(h, w1_ref[...], preferred_element_type=F32)
    u = jnp.dot(h, w3_ref[...], preferred_element_type=F32)
    o_ref[...] += jnp.dot((_silu(g) * u).astype(BF16), w2_ref[...], preferred_element_type=F32)

    @pl.when(j == pl.num_programs(1) - 1)
    def _():
        y = x_ref[...] + 0.5 * gt_ref[...] * o_ref[...]
        if final_norm:
            y = _rms_rows(y, fg_ref[...])
        o_ref[...] = y


def _ffn(x, shift, scale, gate, norm_g, final_g, w13, w2, *, seq_len, final_norm):
    n, d = x.shape
    tm, tf = FFN_ROW_TILE, FFN_HID_TILE
    nf = w2.shape[0] // tf
    seq = lambda i, j: ((i * tm) // seq_len, 0, 0)
    row = lambda i, j: (i, 0)
    const = lambda i, j: (0, 0)
    return pl.pallas_call(
        functools.partial(_ffn_kernel, final_norm=final_norm),
        out_shape=jax.ShapeDtypeStruct((n, d), F32),
        grid=(n // tm, nf),
        in_specs=[pl.BlockSpec((tm, d), row),
                  pl.BlockSpec((None, 1, d), seq),
                  pl.BlockSpec((None, 1, d), seq),
                  pl.BlockSpec((None, 1, d), seq),
                  pl.BlockSpec((1, d), const),
                  pl.BlockSpec((1, d), const),
                  pl.BlockSpec((d, tf), lambda i, j: (0, j)),
                  pl.BlockSpec((d, tf), lambda i, j: (0, nf + j)),
                  pl.BlockSpec((tf, d), lambda i, j: (j, 0))],
        out_specs=pl.BlockSpec((tm, d), row),
        scratch_shapes=[pltpu.VMEM((tm, d), BF16)],
        compiler_params=_params(("parallel", "arbitrary")),
        name="ffn_final" if final_norm else "ffn",
    )(x, shift, scale, gate, norm_g, final_g, w13, w13, w2)


def _inproj_kernel(x_ref, sh_ref, sc_ref, ng_ref, cq_ref, sq_ref, w_ref, o_ref, h_ref):
    _norm_modulate(x_ref, ng_ref, sc_ref, sh_ref, h_ref)
    h = h_ref[...]
    cos, sin = cq_ref[...], sq_ref[...]
    gw = w_ref.shape[1] // N_PROJ
    for j in range(N_PROJ):
        res = jnp.dot(h, w_ref[:, j * gw:(j + 1) * gw], preferred_element_type=F32)
        if j == 0:
            o_ref[j] = (res * (DIFF_QK_DIM ** -0.5 * LOG2E)).astype(BF16)
        elif j in (3, 4):
            scale = 1.0 if j == 3 else HEAD_DIM ** -0.5
            for hd in range(res.shape[1] // HEAD_DIM):
                xs = res[:, hd * HEAD_DIM:(hd + 1) * HEAD_DIM]
                rot = pltpu.roll(xs, HEAD_DIM // 2, 1)
                o_ref[j, :, hd * HEAD_DIM:(hd + 1) * HEAD_DIM] = ((xs * cos + rot * sin) * scale).astype(BF16)
        else:
            o_ref[j] = res.astype(BF16)


def _inproj(x, shift, scale, norm_g, cos_t, sin_t, w_in, *, seq_len):
    n, d = x.shape
    gw = w_in.shape[1] // N_PROJ
    tm = INPROJ_ROW_TILE
    spt = seq_len // tm
    seq = lambda i: ((i * tm) // seq_len, 0, 0)
    return pl.pallas_call(
        _inproj_kernel,
        out_shape=jax.ShapeDtypeStruct((N_PROJ, n, gw), BF16),
        grid=(n // tm,),
        in_specs=[pl.BlockSpec((tm, d), lambda i: (i, 0)),
                  pl.BlockSpec((None, 1, d), seq),
                  pl.BlockSpec((None, 1, d), seq),
                  pl.BlockSpec((1, d), lambda i: (0, 0)),
                  pl.BlockSpec((tm, HEAD_DIM), lambda i: (i % spt, 0)),
                  pl.BlockSpec((tm, HEAD_DIM), lambda i: (i % spt, 0)),
                  pl.BlockSpec(w_in.shape, lambda i: (0, 0), pipeline_mode=pl.Buffered(1))],
        out_specs=pl.BlockSpec((N_PROJ, tm, gw), lambda i: (0, i, 0)),
        scratch_shapes=[pltpu.VMEM((tm, d), BF16)],
        compiler_params=_params(("parallel",)),
        name="inproj",
    )(x, shift, scale, norm_g, cos_t, sin_t, w_in)


def _t5_bucket(rel):
    half = N_BUCKETS // 2
    max_exact = half // 2
    ret = jnp.where(rel > 0, half, 0)
    n = jnp.abs(rel)
    nf = jnp.maximum(n, 1).astype(F32)
    large = max_exact + (jnp.log(nf / max_exact) / math.log(REL_MAX_DIST / max_exact)
                         * (half - max_exact)).astype(jnp.int32)
    large = jnp.minimum(large, half - 1)
    return ret + jnp.where(n < max_exact, n, large)


def _bias_kernel(rb_ref, bk_ref, o_ref):
    h = pl.program_id(0)
    half = N_BUCKETS // 2
    for i in range(o_ref.shape[0]):
        d = i - BIAS_REACH
        if abs(d) > 1:
            b = half - 1 if d < 0 else N_BUCKETS - 1
            o_ref[i] = jnp.full(o_ref.shape[1:], rb_ref[b, h] * LOG2E, F32)
            continue
        bk = bk_ref[d + 1]
        acc = jnp.zeros(bk.shape, F32)
        for b in (range(half) if d < 0 else range(half, N_BUCKETS) if d > 0 else range(N_BUCKETS)):
            acc = jnp.where(bk == b, rb_ref[b, h] * LOG2E, acc)
        o_ref[i] = acc


def _bias_tiles(rel_bias, t):
    assert t >= REL_MAX_DIST
    nh = rel_bias.shape[1]
    i = jnp.arange(t, dtype=jnp.int32)
    rel0 = i[:, None] - i[None, :]
    nt = 2 * BIAS_REACH + 1
    buckets = jnp.stack([_t5_bucket(rel0 + d * t) for d in (-1, 0, 1)])
    return pl.pallas_call(
        _bias_kernel,
        out_shape=jax.ShapeDtypeStruct((nh, nt, t, t), F32),
        grid=(nh,),
        in_specs=[pl.BlockSpec(memory_space=pltpu.SMEM),
                  pl.BlockSpec((3, t, t), lambda h: (0, 0, 0))],
        out_specs=pl.BlockSpec((None, nt, t, t), lambda h: (h, 0, 0, 0)),
        compiler_params=_params(("parallel",)),
        name="bias_tiles",
    )(rel_bias, buckets)


def _attn_kernel(lamp_ref, q_ref, k_ref, v_ref, bias_ref, g_ref, o_ref,
                 vt_ref, qt_ref, u_ref, m_ref, acc_ref, *, t, seq_len, lam_init):
    nk = seq_len // t
    d = HEAD_DIM
    assert nk <= ATTN_MAX_KEY_TILES

    row = lax.broadcasted_iota(jnp.int32, (d, t), 0)
    ksq = jnp.zeros((1, 1), F32)
    for c in range(nk):
        cols = slice(c * t, (c + 1) * t)
        vt_ref[:d, cols] = v_ref[cols, :].astype(F32).T.astype(BF16)
        qt = q_ref[cols, :].astype(F32).T
        for idx, qmask in enumerate((row < DIFF_QK_DIM, row >= DIFF_QK_DIM)):
            qh = jnp.where(qmask, qt, 0.0)
            qt_ref[idx, :, cols] = qh.astype(BF16)
            u_ref[idx, :, cols] = jnp.sum(qh * qh, axis=0, keepdims=True)
        kn = jnp.sum(jnp.square(k_ref[cols, :].astype(F32)), axis=1, keepdims=True)
        ksq = jnp.maximum(ksq, jnp.max(kn, axis=0, keepdims=True))
    vt_ref[d:, :] = jnp.ones((SUM_ROWS, seq_len), BF16)
    center = bias_ref[BIAS_REACH]
    bmax = jnp.max(jnp.max(center, axis=0, keepdims=True), axis=1, keepdims=True)
    u_ref[...] = jnp.sqrt(u_ref[...] * ksq) + (bmax + ATTN_BOUND_SLACK)

    lp = lamp_ref[...]
    lam = (jnp.exp(jnp.sum(lp[0:1] * lp[1:2], axis=1, keepdims=True))
           - jnp.exp(jnp.sum(lp[2:3] * lp[3:4], axis=1, keepdims=True)) + lam_init)

    def tile(i):
        return pl.ds(i * t, t) if isinstance(i, int) else pl.ds(pl.multiple_of(i * t, t), t)

    def tile_operands(qi, j):
        kt = k_ref[tile(j), :]
        r = t // BIAS_TILE
        bias = jnp.concatenate([jnp.concatenate(
            [bias_ref[jnp.clip(r * (j - qi) + a - b, -BIAS_REACH, BIAS_REACH) + BIAS_REACH] for b in range(r)],
            axis=1) for a in range(r)], axis=0)
        return kt, bias

    def queries(ref, idx, qi):
        return ref[idx, :, pl.ds(pl.multiple_of(qi * t, t), t)]

    def values(j):
        return vt_ref[:, tile(j)]

    def bounded_tile(qi):
        def accumulate(j, ps):
            vt = values(j)
            for idx in range(2):
                pv = jnp.dot(vt, ps[idx], preferred_element_type=F32)
                acc_ref[idx] = pv if j == 0 else acc_ref[idx] + pv

        pending = None
        for j in range(nk):
            kt, bias = tile_operands(qi, j)
            ps = []
            for idx in range(2):
                s = jnp.dot(kt, queries(qt_ref, idx, qi), preferred_element_type=F32)
                ps.append(jnp.exp2(s + bias - queries(u_ref, idx, qi)).astype(BF16))
            if pending is not None:
                accumulate(*pending)
            pending = (j, ps)
        accumulate(*pending)

    def running_max_tile(qi):
        m_ref[...] = jnp.full(m_ref.shape, -jnp.inf, F32)
        acc_ref[...] = jnp.zeros(acc_ref.shape, F32)

        def key_tile(j, c):
            kt, bias = tile_operands(qi, j)
            vt = values(j)
            for idx in range(2):
                s = jnp.dot(kt, queries(qt_ref, idx, qi), preferred_element_type=F32) + bias
                m_old = m_ref[idx]
                m_new = jnp.maximum(m_old, jnp.max(s, axis=0, keepdims=True))
                p = jnp.exp2(s - m_new).astype(BF16)
                m_ref[idx] = m_new
                acc_ref[idx] = jnp.exp2(m_old - m_new) * acc_ref[idx] + jnp.dot(vt, p, preferred_element_type=F32)
            return c

        lax.fori_loop(0, nk, key_tile, 0)

    def query_tile(qi, carry):
        bounded_tile(qi)
        denom = jnp.minimum(acc_ref[0, d:d + 1], acc_ref[1, d:d + 1])

        @pl.when(jnp.logical_not(jnp.min(denom) >= ATTN_MIN_DENOM))
        def _():
            running_max_tile(qi)

        out = acc_ref[0, :d] / acc_ref[0, d:d + 1] - lam * (acc_ref[1, :d] / acc_ref[1, d:d + 1])
        y = out * lax.rsqrt(jnp.mean(out * out, axis=0, keepdims=True) + EPS)
        o_ref[pl.ds(pl.multiple_of(qi * t, t), t), :] = (y.T * g_ref[...] * (1.0 - lam_init)).astype(BF16)
        return carry

    lax.fori_loop(0, nk, query_tile, 0)


def _diff_attention(proj, bias_tiles, lam_params, head_g, *, batch, seq_len, lam_init):
    n = proj.shape[1]
    nh = proj.shape[2] // HEAD_DIM
    t = ATTN_TILE
    nt = bias_tiles.shape[1]
    blk = lambda g: pl.BlockSpec((None, seq_len, HEAD_DIM), lambda h, b: (g, b, h))
    return pl.pallas_call(
        functools.partial(_attn_kernel, t=t, seq_len=seq_len, lam_init=lam_init),
        out_shape=jax.ShapeDtypeStruct((n, nh * HEAD_DIM), BF16),
        grid=(nh, batch),
        in_specs=[pl.BlockSpec(lam_params.shape, lambda h, b: (0, 0)),
                  blk(0), blk(1), blk(2),
                  pl.BlockSpec((None, nt, BIAS_TILE, BIAS_TILE), lambda h, b: (h, 0, 0, 0)),
                  pl.BlockSpec((1, HEAD_DIM), lambda h, b: (0, 0))],
        out_specs=pl.BlockSpec((seq_len, HEAD_DIM), lambda h, b: (b, h)),
        scratch_shapes=[pltpu.VMEM((HEAD_DIM + SUM_ROWS, seq_len), BF16),
                        pltpu.VMEM((2, HEAD_DIM, seq_len), BF16),
                        pltpu.VMEM((2, 1, seq_len), F32),
                        pltpu.VMEM((2, 1, t), F32),
                        pltpu.VMEM((2, HEAD_DIM + SUM_ROWS, t), F32)],
        compiler_params=_params(("parallel", "parallel")),
        name="diff_attention",
    )(lam_params, proj, proj, proj, bias_tiles, head_g)


def _log_sigmoid(x):
    return jnp.minimum(x, 0.0) - jnp.log1p(jnp.exp(-jnp.abs(x)))


def _ret_kernel(df_ref, db_ref, q_ref, k_ref, v_ref, rg_ref, g_ref, o_ref, u_ref, r_ref, *, seq_len, c):
    h = pl.program_id(1)
    n = seq_len // c
    d = HEAD_DIM
    df, db = df_ref[0, h], db_ref[0, h]

    lgf = _log_sigmoid(jnp.full((c, d), df, F32))
    lgb = _log_sigmoid(jnp.full((c, d), db, F32))
    ri = lax.broadcasted_iota(jnp.int32, (c, d), 0).astype(F32)
    xi_f, ze_f = jnp.exp(lgf * (ri + 1.0)), jnp.exp(lgf * (c - 1.0 - ri))
    xi_b, ze_b = jnp.exp(lgb * (c - ri)), jnp.exp(lgb * ri)
    gc_f = jnp.exp(_log_sigmoid(jnp.full((d, d), df, F32)) * c)
    gc_b = jnp.exp(_log_sigmoid(jnp.full((d, d), db, F32)) * c)

    diff = (lax.broadcasted_iota(jnp.int32, (c, c), 0) - lax.broadcasted_iota(jnp.int32, (c, c), 1)).astype(F32)
    lgf2 = _log_sigmoid(jnp.full((c, c), df, F32))
    lgb2 = _log_sigmoid(jnp.full((c, c), db, F32))
    decay = jnp.where(diff >= 0, jnp.exp(lgf2 * jnp.maximum(diff, 0.0)), jnp.exp(lgb2 * jnp.maximum(-diff, 0.0)))

    def chunk(ref, i):
        return ref[pl.ds(pl.multiple_of(i * c, c), c), :]

    def local_state(i, carry):
        kc = chunk(k_ref, i).astype(F32)
        kz = jnp.concatenate([kc * ze_f, kc * ze_b], axis=1).astype(BF16)
        u_ref[i] = lax.dot_general(kz, chunk(v_ref, i), (((0,), (0,)), ((), ())), preferred_element_type=F32)
        return carry

    lax.fori_loop(0, n, local_state, 0, unroll=RET_UNROLL)

    def scan_fwd(i, r):
        r_ref[i, :d, :] = r.astype(BF16)
        return gc_f * r + u_ref[i, :d, :]

    def scan_bwd(i, r):
        i = n - 1 - i
        r_ref[i, d:, :] = r.astype(BF16)
        return gc_b * r + u_ref[i, d:, :]

    lax.fori_loop(0, n, scan_fwd, jnp.zeros((d, d), F32))
    lax.fori_loop(0, n, scan_bwd, jnp.zeros((d, d), F32))

    def outputs(i, carry):
        qb = chunk(q_ref, i)
        qc = qb.astype(F32)
        vc = chunk(v_ref, i)
        qx = jnp.concatenate([qc * xi_f, qc * xi_b], axis=1).astype(BF16)
        inter = jnp.dot(qx, r_ref[i], preferred_element_type=F32)
        inner = lax.dot_general(qb, chunk(k_ref, i), (((1,), (1,)), ((), ())), preferred_element_type=F32) * decay
        o = inter + jnp.dot(inner.astype(BF16), vc, preferred_element_type=F32)
        rg = chunk(rg_ref, i).astype(F32)
        o_ref[pl.ds(pl.multiple_of(i * c, c), c), :] = (_rms_rows(o, g_ref[...]) * _silu(rg)).astype(BF16)
        return carry

    lax.fori_loop(0, n, outputs, 0, unroll=RET_UNROLL)


def _retention(proj, decay_f, decay_b, head_g, *, batch, seq_len):
    n = proj.shape[1]
    nh = proj.shape[2] // HEAD_DIM
    c = RET_CHUNK
    blk = lambda g: pl.BlockSpec((None, seq_len, HEAD_DIM), lambda b, h: (g, b, h))
    smem = pl.BlockSpec(memory_space=pltpu.SMEM)
    return pl.pallas_call(
        functools.partial(_ret_kernel, seq_len=seq_len, c=c),
        out_shape=jax.ShapeDtypeStruct((n, nh * HEAD_DIM), BF16),
        grid=(batch, nh),
        in_specs=[smem, smem, blk(3), blk(4), blk(5), blk(6),
                  pl.BlockSpec((1, HEAD_DIM), lambda b, h: (0, 0))],
        out_specs=pl.BlockSpec((seq_len, HEAD_DIM), lambda b, h: (b, h)),
        scratch_shapes=[pltpu.VMEM((seq_len // c, 2 * HEAD_DIM, HEAD_DIM), F32),
                        pltpu.VMEM((seq_len // c, 2 * HEAD_DIM, HEAD_DIM), BF16)],
        compiler_params=_params(("parallel", "parallel")),
        name="retention",
    )(decay_f, decay_b, proj, proj, proj, proj, head_g)


def _outproj_kernel(x_ref, gt_ref, a_ref, r_ref, wa_ref, wr_ref, o_ref):
    y = (jnp.dot(a_ref[...], wa_ref[...], preferred_element_type=F32)
         + jnp.dot(r_ref[...], wr_ref[...], preferred_element_type=F32))
    o_ref[...] = x_ref[...] + gt_ref[...] * y


def _outproj(x, gate, d_out, r_out, w_out, *, seq_len):
    n, d = x.shape
    kw = d_out.shape[1]
    tm = PROJ_ROW_TILE
    row = lambda i: (i, 0)
    return pl.pallas_call(
        _outproj_kernel,
        out_shape=jax.ShapeDtypeStruct((n, d), F32),
        grid=(n // tm,),
        in_specs=[pl.BlockSpec((tm, d), row),
                  pl.BlockSpec((None, 1, d), lambda i: ((i * tm) // seq_len, 0, 0)),
                  pl.BlockSpec((tm, kw), row),
                  pl.BlockSpec((tm, kw), row),
                  pl.BlockSpec((kw, d), lambda i: (0, 0)),
                  pl.BlockSpec((kw, d), lambda i: (1, 0))],
        out_specs=pl.BlockSpec((tm, d), row),
        compiler_params=_params(("parallel",)),
        name="outproj",
    )(x, gate, d_out, r_out, w_out, w_out)


def _rotary_tables(seq_len):
    pos = jnp.arange(seq_len, dtype=F32)
    inv = ROPE_BASE ** (-jnp.arange(0, HEAD_DIM, 2, dtype=F32) / HEAD_DIM)
    ang = pos[:, None] * inv[None, :]
    cos, sin = jnp.cos(ang), jnp.sin(ang)
    return jnp.concatenate([cos, cos], axis=1), jnp.concatenate([-sin, sin], axis=1)


def _trunk(x, mod, w, bias_tiles):
    batch, seq_len, d = x.shape
    x = x.reshape(batch * seq_len, d)
    sh1, sc1, g1, shm, scm, gm, sh2, sc2, g2 = [mod[:, i].reshape(batch, 1, d) for i in range(N_MOD)]
    lam_init = 0.8 - 0.6 * math.exp(-0.3 * 0)
    cos_t, sin_t = _rotary_tables(seq_len)

    x = _ffn(x, sh1, sc1, g1, w["ffn1_norm_g"], w["final_norm_g"], w["ffn1_w13"], w["ffn1_w2"],
             seq_len=seq_len, final_norm=False)
    proj = _inproj(x, shm, scm, w["mix_norm_g"], cos_t, sin_t, w["w_in"], seq_len=seq_len)
    d_out = _diff_attention(proj, bias_tiles, w["lam_params"], w["diff_head_g"],
                            batch=batch, seq_len=seq_len, lam_init=lam_init)
    r_out = _retention(proj, w["ret_decay_fwd"], w["ret_decay_bwd"], w["ret_head_g"],
                       batch=batch, seq_len=seq_len)
    x = _outproj(x, gm, d_out, r_out, w["w_out"], seq_len=seq_len)
    x = _ffn(x, sh2, sc2, g2, w["ffn2_norm_g"], w["final_norm_g"], w["ffn2_w13"], w["ffn2_w2"],
             seq_len=seq_len, final_norm=True)
    return x.reshape(batch, seq_len, d)


def kernel(x_prompt, x_sample, c_prompt, c_sample, ada_w, ada_b, ffn1_norm_g, ffn1_w13, ffn1_w2, mix_norm_g, w_in, diff_lambda_q1, diff_lambda_k1, diff_lambda_q2, diff_lambda_k2, diff_head_g, rel_bias, ret_decay_fwd, ret_decay_bwd, ret_head_g, w_out, ffn2_norm_g, ffn2_w13, ffn2_w2, final_norm_g):
    assert ada_w.shape[0] == 1, "single-layer trunk"
    d = x_prompt.shape[-1]
    nb = c_prompt.shape[0]
    row = lambda a: a.reshape(1, -1)
    w = {
        "ffn1_norm_g": ffn1_norm_g, "mix_norm_g": mix_norm_g, "ffn2_norm_g": ffn2_norm_g,
        "final_norm_g": row(final_norm_g), "diff_head_g": diff_head_g, "ret_head_g": ret_head_g,
        "ffn1_w13": ffn1_w13[0].astype(BF16), "ffn1_w2": ffn1_w2[0].astype(BF16),
        "ffn2_w13": ffn2_w13[0].astype(BF16), "ffn2_w2": ffn2_w2[0].astype(BF16),
        "w_in": w_in[0].astype(BF16), "w_out": w_out[0].astype(BF16),
        "ret_decay_fwd": ret_decay_fwd, "ret_decay_bwd": ret_decay_bwd,
        "lam_params": jnp.concatenate([diff_lambda_q1, diff_lambda_k1, diff_lambda_q2, diff_lambda_k2], axis=0),
    }
    mod = _ada_mod(jnp.concatenate([c_prompt, c_sample], axis=0), ada_w[0], ada_b[0])
    mod = mod.reshape(mod.shape[0], N_MOD, d)
    bias_tiles = _bias_tiles(rel_bias, BIAS_TILE)
    y_prompt = _trunk(x_prompt, mod[:nb], w, bias_tiles)
    y_sample = _trunk(x_sample, mod[nb:], w, bias_tiles)
    return (y_prompt, y_sample)
```

```python
import functools
import math

import jax
import jax.numpy as jnp
from jax import lax
from jax.experimental import pallas as pl
from jax.experimental.pallas import tpu as pltpu

F32 = jnp.float32
BF16 = jnp.bfloat16

HEAD_DIM = 128
DIFF_QK_DIM = HEAD_DIM // 2
N_BUCKETS = 32
REL_MAX_DIST = 128
ROPE_BASE = 10000.0
N_MOD = 9
EPS = 1e-6
LOG2E = math.log2(math.e)
N_PROJ = 7

V7X_VMEM_BYTES = 64 * 1024 * 1024
VMEM_LIMIT = 56 * 1024 * 1024

FFN_ROW_TILE = 512
FFN_HID_TILE = 512
INPROJ_ROW_TILE = 256
PROJ_ROW_TILE = 512
NORM_ROW_BLOCK = 64
NORM_UNROLL = 2
ATTN_TILE = 1024
BIAS_TILE = 512
ATTN_MAX_KEY_TILES = 8
ATTN_BOUND_SLACK = 1.0
ATTN_MIN_DENOM = 2.0 ** -60
SUM_ROWS = 16
BIAS_REACH = 2
RET_CHUNK = 256
RET_UNROLL = 8
ADA_COL_TILE = 1024


def _params(sem):
    return pltpu.CompilerParams(dimension_semantics=sem, vmem_limit_bytes=VMEM_LIMIT)


def _rms_rows(x, g):
    return x * lax.rsqrt(jnp.mean(x * x, axis=-1, keepdims=True) + EPS) * g


def _silu(x):
    return x * jax.nn.sigmoid(x)


def _norm_modulate(x_ref, ng_ref, sc_ref, sh_ref, h_ref):
    gain = ng_ref[...] * (1.0 + sc_ref[...])
    shift = sh_ref[...]

    def block(i, carry):
        rows = pl.ds(pl.multiple_of(i * NORM_ROW_BLOCK, NORM_ROW_BLOCK), NORM_ROW_BLOCK)
        x = x_ref[rows, :]
        r = lax.rsqrt(jnp.mean(x * x, axis=-1, keepdims=True) + EPS)
        h_ref[rows, :] = (x * r * gain + shift).astype(BF16)
        return carry

    lax.fori_loop(0, x_ref.shape[0] // NORM_ROW_BLOCK, block, 0, unroll=NORM_UNROLL)


def _ada_kernel(c_ref, w_ref, b_ref, o_ref):
    a = _silu(c_ref[...]).astype(BF16)
    o_ref[...] = jnp.dot(a, w_ref[...].astype(BF16), preferred_element_type=F32) + b_ref[...]


def _ada_mod(c, w, b):
    nb, d = c.shape
    n = w.shape[1]
    tn = ADA_COL_TILE
    return pl.pallas_call(
        _ada_kernel,
        out_shape=jax.ShapeDtypeStruct((nb, n), F32),
        grid=(n // tn,),
        in_specs=[pl.BlockSpec((nb, d), lambda j: (0, 0)),
                  pl.BlockSpec((d, tn), lambda j: (0, j)),
                  pl.BlockSpec((1, tn), lambda j: (0, j))],
        out_specs=pl.BlockSpec((nb, tn), lambda j: (0, j)),
        compiler_params=_params(("parallel",)),
        name="ada_mod",
    )(c, w, b.reshape(1, n))


def _ffn_kernel(x_ref, sh_ref, sc_ref, gt_ref, ng_ref, fg_ref, w1_ref, w3_ref, w2_ref,
                o_ref, h_ref, *, final_norm):
    j = pl.program_id(1)

    @pl.when(j == 0)
    def _():
        _norm_modulate(x_ref, ng_ref, sc_ref, sh_ref, h_ref)
        o_ref[...] = jnp.zeros(o_ref.shape, F32)

    h = h_ref[...]
    g = jnp.dot(h, w1_ref[...], preferred_element_type=F32)
    u = jnp.dot(h, w3_ref[...], preferred_element_type=F32)
    o_ref[...] += jnp.dot((_silu(g) * u).astype(BF16), w2_ref[...], preferred_element_type=F32)

    @pl.when(j == pl.num_programs(1) - 1)
    def _():
        y = x_ref[...] + 0.5 * gt_ref[...] * o_ref[...]
        if final_norm:
            y = _rms_rows(y, fg_ref[...])
        o_ref[...] = y


def _ffn(x, shift, scale, gate, norm_g, final_g, w13, w2, *, seq_len, final_norm):
    n, d = x.shape
    tm, tf = FFN_ROW_TILE, FFN_HID_TILE
    nf = w2.shape[0] // tf
    seq = lambda i, j: ((i * tm) // seq_len, 0, 0)
    row = lambda i, j: (i, 0)
    const = lambda i, j: (0, 0)
    return pl.pallas_call(
        functools.partial(_ffn_kernel, final_norm=final_norm),
        out_shape=jax.ShapeDtypeStruct((n, d), F32),
        grid=(n // tm, nf),
        in_specs=[pl.BlockSpec((tm, d), row),
                  pl.BlockSpec((None, 1, d), seq),
                  pl.BlockSpec((None, 1, d), seq),
                  pl.BlockSpec((None, 1, d), seq),
                  pl.BlockSpec((1, d), const),
                  pl.BlockSpec((1, d), const),
                  pl.BlockSpec((d, tf), lambda i, j: (0, j)),
                  pl.BlockSpec((d, tf), lambda i, j: (0, nf + j)),
                  pl.BlockSpec((tf, d), lambda i, j: (j, 0))],
        out_specs=pl.BlockSpec((tm, d), row),
        scratch_shapes=[pltpu.VMEM((tm, d), BF16)],
        compiler_params=_params(("parallel", "arbitrary")),
        name="ffn_final" if final_norm else "ffn",
    )(x, shift, scale, gate, norm_g, final_g, w13, w13, w2)


def _inproj_kernel(x_ref, sh_ref, sc_ref, ng_ref, cq_ref, sq_ref, w_ref, o_ref, h_ref):
    _norm_modulate(x_ref, ng_ref, sc_ref, sh_ref, h_ref)
    h = h_ref[...]
    cos, sin = cq_ref[...], sq_ref[...]
    gw = w_ref.shape[1] // N_PROJ
    for j in range(N_PROJ):
        res = jnp.dot(h, w_ref[:, j * gw:(j + 1) * gw], preferred_element_type=F32)
        if j == 0:
            o_ref[j] = (res * (DIFF_QK_DIM ** -0.5 * LOG2E)).astype(BF16)
        elif j in (3, 4):
            scale = 1.0 if j == 3 else HEAD_DIM ** -0.5
            for hd in range(res.shape[1] // HEAD_DIM):
                xs = res[:, hd * HEAD_DIM:(hd + 1) * HEAD_DIM]
                rot = pltpu.roll(xs, HEAD_DIM // 2, 1)
                o_ref[j, :, hd * HEAD_DIM:(hd + 1) * HEAD_DIM] = ((xs * cos + rot * sin) * scale).astype(BF16)
        else:
            o_ref[j] = res.astype(BF16)


def _inproj(x, shift, scale, norm_g, cos_t, sin_t, w_in, *, seq_len):
    n, d = x.shape
    gw = w_in.shape[1] // N_PROJ
    tm = INPROJ_ROW_TILE
    spt = seq_len // tm
    seq = lambda i: ((i * tm) // seq_len, 0, 0)
    return pl.pallas_call(
        _inproj_kernel,
        out_shape=jax.ShapeDtypeStruct((N_PROJ, n, gw), BF16),
        grid=(n // tm,),
        in_specs=[pl.BlockSpec((tm, d), lambda i: (i, 0)),
                  pl.BlockSpec((None, 1, d), seq),
                  pl.BlockSpec((None, 1, d), seq),
                  pl.BlockSpec((1, d), lambda i: (0, 0)),
                  pl.BlockSpec((tm, HEAD_DIM), lambda i: (i % spt, 0)),
                  pl.BlockSpec((tm, HEAD_DIM), lambda i: (i % spt, 0)),
                  pl.BlockSpec(w_in.shape, lambda i: (0, 0), pipeline_mode=pl.Buffered(1))],
        out_specs=pl.BlockSpec((N_PROJ, tm, gw), lambda i: (0, i, 0)),
        scratch_shapes=[pltpu.VMEM((tm, d), BF16)],
        compiler_params=_params(("parallel",)),
        name="inproj",
    )(x, shift, scale, norm_g, cos_t, sin_t, w_in)


def _t5_bucket(rel):
    half = N_BUCKETS // 2
    max_exact = half // 2
    ret = jnp.where(rel > 0, half, 0)
    n = jnp.abs(rel)
    nf = jnp.maximum(n, 1).astype(F32)
    large = max_exact + (jnp.log(nf / max_exact) / math.log(REL_MAX_DIST / max_exact)
                         * (half - max_exact)).astype(jnp.int32)
    large = jnp.minimum(large, half - 1)
    return ret + jnp.where(n < max_exact, n, large)


def _bias_kernel(rb_ref, bk_ref, o_ref):
    h = pl.program_id(0)
    half = N_BUCKETS // 2
    for i in range(o_ref.shape[0]):
        d = i - BIAS_REACH
        if abs(d) > 1:
            b = half - 1 if d < 0 else N_BUCKETS - 1
            o_ref[i] = jnp.full(o_ref.shape[1:], rb_ref[b, h] * LOG2E, F32)
            continue
        bk = bk_ref[d + 1]
        acc = jnp.zeros(bk.shape, F32)
        for b in (range(half) if d < 0 else range(half, N_BUCKETS) if d > 0 else range(N_BUCKETS)):
            acc = jnp.where(bk == b, rb_ref[b, h] * LOG2E, acc)
        o_ref[i] = acc


def _bias_tiles(rel_bias, t):
    assert t >= REL_MAX_DIST
    nh = rel_bias.shape[1]
    i = jnp.arange(t, dtype=jnp.int32)
    rel0 = i[:, None] - i[None, :]
    nt = 2 * BIAS_REACH + 1
    buckets = jnp.stack([_t5_bucket(rel0 + d * t) for d in (-1, 0, 1)])
    return pl.pallas_call(
        _bias_kernel,
        out_shape=jax.ShapeDtypeStruct((nh, nt, t, t), F32),
        grid=(nh,),
        in_specs=[pl.BlockSpec(memory_space=pltpu.SMEM),
                  pl.BlockSpec((3, t, t), lambda h: (0, 0, 0))],
        out_specs=pl.BlockSpec((None, nt, t, t), lambda h: (h, 0, 0, 0)),
        compiler_params=_params(("parallel",)),
        name="bias_tiles",
    )(rel_bias, buckets)


def _attn_kernel(lamp_ref, q_ref, k_ref, v_ref, bias_ref, g_ref, o_ref,
                 vt_ref, qt_ref, u_ref, m_ref, acc_ref, *, t, seq_len, lam_init):
    nk = seq_len // t
    d = HEAD_DIM
    assert nk <= ATTN_MAX_KEY_TILES

    row = lax.broadcasted_iota(jnp.int32, (d, t), 0)
    eye = (lax.broadcasted_iota(jnp.int32, (d, d), 0) == lax.broadcasted_iota(jnp.int32, (d, d), 1)).astype(BF16)
    transpose = lambda x: lax.dot_general(eye, x, (((1,), (1,)), ((), ())), preferred_element_type=F32)
    ksq = jnp.zeros((1, 1), F32)
    for c in range(nk):
        cols = slice(c * t, (c + 1) * t)
        vt_ref[:d, cols] = transpose(v_ref[cols, :]).astype(BF16)
        qt = transpose(q_ref[cols, :])
        for idx, qmask in enumerate((row < DIFF_QK_DIM, row >= DIFF_QK_DIM)):
            qh = jnp.where(qmask, qt, 0.0)
            qt_ref[idx, :, cols] = qh.astype(BF16)
            u_ref[idx, :, cols] = jnp.sum(qh * qh, axis=0, keepdims=True)
        kn = jnp.sum(jnp.square(k_ref[cols, :].astype(F32)), axis=1, keepdims=True)
        ksq = jnp.maximum(ksq, jnp.max(kn, axis=0, keepdims=True))
    vt_ref[d:, :] = jnp.ones((SUM_ROWS, seq_len), BF16)
    center = bias_ref[BIAS_REACH]
    bmax = jnp.max(jnp.max(center, axis=0, keepdims=True), axis=1, keepdims=True)
    u_ref[...] = jnp.sqrt(u_ref[...] * ksq) + (bmax + ATTN_BOUND_SLACK)

    lp = lamp_ref[...]
    lam = (jnp.exp(jnp.sum(lp[0:1] * lp[1:2], axis=1, keepdims=True))
           - jnp.exp(jnp.sum(lp[2:3] * lp[3:4], axis=1, keepdims=True)) + lam_init)

    def tile(i):
        return pl.ds(i * t, t) if isinstance(i, int) else pl.ds(pl.multiple_of(i * t, t), t)

    def tile_operands(qi, j):
        kt = k_ref[tile(j), :]
        r = t // BIAS_TILE
        bias = jnp.concatenate([jnp.concatenate(
            [bias_ref[jnp.clip(r * (j - qi) + a - b, -BIAS_REACH, BIAS_REACH) + BIAS_REACH] for b in range(r)],
            axis=1) for a in range(r)], axis=0)
        return kt, bias

    def queries(ref, idx, qi):
        return ref[idx, :, pl.ds(pl.multiple_of(qi * t, t), t)]

    def values(j):
        return vt_ref[:, tile(j)]

    def bounded_tile(qi):
        def accumulate(j, ps):
            vt = values(j)
            for idx in range(2):
                pv = jnp.dot(vt, ps[idx], preferred_element_type=F32)
                acc_ref[idx] = pv if j == 0 else acc_ref[idx] + pv

        pending = None
        for j in range(nk):
            kt, bias = tile_operands(qi, j)
            ps = []
            for idx in range(2):
                s = jnp.dot(kt, queries(qt_ref, idx, qi), preferred_element_type=F32)
                ps.append(jnp.exp2(s + bias - queries(u_ref, idx, qi)).astype(BF16))
            if pending is not None:
                accumulate(*pending)
            pending = (j, ps)
        accumulate(*pending)

    def running_max_tile(qi):
        m_ref[...] = jnp.full(m_ref.shape, -jnp.inf, F32)
        acc_ref[...] = jnp.zeros(acc_ref.shape, F32)

        def key_tile(j, c):
            kt, bias = tile_operands(qi, j)
            vt = values(j)
            for idx in range(2):
                s = jnp.dot(kt, queries(qt_ref, idx, qi), preferred_element_type=F32) + bias
                m_old = m_ref[idx]
                m_new = jnp.maximum(m_old, jnp.max(s, axis=0, keepdims=True))
                p = jnp.exp2(s - m_new).astype(BF16)
                m_ref[idx] = m_new
                acc_ref[idx] = jnp.exp2(m_old - m_new) * acc_ref[idx] + jnp.dot(vt, p, preferred_element_type=F32)
            return c

        lax.fori_loop(0, nk, key_tile, 0)

    def query_tile(qi, carry):
        bounded_tile(qi)
        denom = jnp.minimum(acc_ref[0, d:d + 1], acc_ref[1, d:d + 1])

        @pl.when(jnp.logical_not(jnp.min(denom) >= ATTN_MIN_DENOM))
        def _():
            running_max_tile(qi)

        out = acc_ref[0, :d] / acc_ref[0, d:d + 1] - lam * (acc_ref[1, :d] / acc_ref[1, d:d + 1])
        y = out * lax.rsqrt(jnp.mean(out * out, axis=0, keepdims=True) + EPS)
        o_ref[pl.ds(pl.multiple_of(qi * t, t), t), :] = (y.T * g_ref[...] * (1.0 - lam_init)).astype(BF16)
        return carry

    lax.fori_loop(0, nk, query_tile, 0)


def _diff_attention(proj, bias_tiles, lam_params, head_g, *, batch, seq_len, lam_init):
    n = proj.shape[1]
    nh = proj.shape[2] // HEAD_DIM
    t = ATTN_TILE
    nt = bias_tiles.shape[1]
    blk = lambda g: pl.BlockSpec((None, seq_len, HEAD_DIM), lambda h, b: (g, b, h))
    return pl.pallas_call(
        functools.partial(_attn_kernel, t=t, seq_len=seq_len, lam_init=lam_init),
        out_shape=jax.ShapeDtypeStruct((n, nh * HEAD_DIM), BF16),
        grid=(nh, batch),
        in_specs=[pl.BlockSpec(lam_params.shape, lambda h, b: (0, 0)),
                  blk(0), blk(1), blk(2),
                  pl.BlockSpec((None, nt, BIAS_TILE, BIAS_TILE), lambda h, b: (h, 0, 0, 0)),
                  pl.BlockSpec((1, HEAD_DIM), lambda h, b: (0, 0))],
        out_specs=pl.BlockSpec((seq_len, HEAD_DIM), lambda h, b: (b, h)),
        scratch_shapes=[pltpu.VMEM((HEAD_DIM + SUM_ROWS, seq_len), BF16),
                        pltpu.VMEM((2, HEAD_DIM, seq_len), BF16),
                        pltpu.VMEM((2, 1, seq_len), F32),
                        pltpu.VMEM((2, 1, t), F32),
                        pltpu.VMEM((2, HEAD_DIM + SUM_ROWS, t), F32)],
        compiler_params=_params(("parallel", "parallel")),
        name="diff_attention",
    )(lam_params, proj, proj, proj, bias_tiles, head_g)


def _log_sigmoid(x):
    return jnp.minimum(x, 0.0) - jnp.log1p(jnp.exp(-jnp.abs(x)))


def _ret_kernel(df_ref, db_ref, q_ref, k_ref, v_ref, rg_ref, g_ref, o_ref, u_ref, r_ref, *, seq_len, c):
    h = pl.program_id(1)
    n = seq_len // c
    d = HEAD_DIM
    df, db = df_ref[0, h], db_ref[0, h]

    lgf = _log_sigmoid(jnp.full((c, d), df, F32))
    lgb = _log_sigmoid(jnp.full((c, d), db, F32))
    ri = lax.broadcasted_iota(jnp.int32, (c, d), 0).astype(F32)
    xi_f, ze_f = jnp.exp(lgf * (ri + 1.0)), jnp.exp(lgf * (c - 1.0 - ri))
    xi_b, ze_b = jnp.exp(lgb * (c - ri)), jnp.exp(lgb * ri)
    gc_f = jnp.exp(_log_sigmoid(jnp.full((d, d), df, F32)) * c)
    gc_b = jnp.exp(_log_sigmoid(jnp.full((d, d), db, F32)) * c)

    diff = (lax.broadcasted_iota(jnp.int32, (c, c), 0) - lax.broadcasted_iota(jnp.int32, (c, c), 1)).astype(F32)
    lgf2 = _log_sigmoid(jnp.full((c, c), df, F32))
    lgb2 = _log_sigmoid(jnp.full((c, c), db, F32))
    decay = jnp.where(diff >= 0, jnp.exp(lgf2 * jnp.maximum(diff, 0.0)), jnp.exp(lgb2 * jnp.maximum(-diff, 0.0)))

    def chunk(ref, i):
        return ref[pl.ds(pl.multiple_of(i * c, c), c), :]

    def local_state(i, carry):
        kc = chunk(k_ref, i).astype(F32)
        kz = jnp.concatenate([kc * ze_f, kc * ze_b], axis=1).astype(BF16)
        u_ref[i] = lax.dot_general(kz, chunk(v_ref, i), (((0,), (0,)), ((), ())), preferred_element_type=F32)
        return carry

    lax.fori_loop(0, n, local_state, 0, unroll=RET_UNROLL)

    def scan_fwd(i, r):
        r_ref[i, :d, :] = r.astype(BF16)
        return gc_f * r + u_ref[i, :d, :]

    def scan_bwd(i, r):
        i = n - 1 - i
        r_ref[i, d:, :] = r.astype(BF16)
        return gc_b * r + u_ref[i, d:, :]

    lax.fori_loop(0, n, scan_fwd, jnp.zeros((d, d), F32))
    lax.fori_loop(0, n, scan_bwd, jnp.zeros((d, d), F32))

    def outputs(i, carry):
        qb = chunk(q_ref, i)
        qc = qb.astype(F32)
        vc = chunk(v_ref, i)
        qx = jnp.concatenate([qc * xi_f, qc * xi_b], axis=1).astype(BF16)
        inter = jnp.dot(qx, r_ref[i], preferred_element_type=F32)
        inner = lax.dot_general(qb, chunk(k_ref, i), (((1,), (1,)), ((), ())), preferred_element_type=F32) * decay
        o = inter + jnp.dot(inner.astype(BF16), vc, preferred_element_type=F32)
        rg = chunk(rg_ref, i).astype(F32)
        o_ref[pl.ds(pl.multiple_of(i * c, c), c), :] = (_rms_rows(o, g_ref[...]) * _silu(rg)).astype(BF16)
        return carry

    lax.fori_loop(0, n, outputs, 0, unroll=RET_UNROLL)


def _retention(proj, decay_f, decay_b, head_g, *, batch, seq_len):
    n = proj.shape[1]
    nh = proj.shape[2] // HEAD_DIM
    c = RET_CHUNK
    blk = lambda g: pl.BlockSpec((None, seq_len, HEAD_DIM), lambda b, h: (g, b, h))
    smem = pl.BlockSpec(memory_space=pltpu.SMEM)
    return pl.pallas_call(
        functools.partial(_ret_kernel, seq_len=seq_len, c=c),
        out_shape=jax.ShapeDtypeStruct((n, nh * HEAD_DIM), BF16),
        grid=(batch, nh),
        in_specs=[smem, smem, blk(3), blk(4), blk(5), blk(6),
                  pl.BlockSpec((1, HEAD_DIM), lambda b, h: (0, 0))],
        out_specs=pl.BlockSpec((seq_len, HEAD_DIM), lambda b, h: (b, h)),
        scratch_shapes=[pltpu.VMEM((seq_len // c, 2 * HEAD_DIM, HEAD_DIM), F32),
                        pltpu.VMEM((seq_len // c, 2 * HEAD_DIM, HEAD_DIM), BF16)],
        compiler_params=_params(("parallel", "parallel")),
        name="retention",
    )(decay_f, decay_b, proj, proj, proj, proj, head_g)


def _outproj_kernel(x_ref, gt_ref, a_ref, r_ref, wa_ref, wr_ref, o_ref):
    y = (jnp.dot(a_ref[...], wa_ref[...], preferred_element_type=F32)
         + jnp.dot(r_ref[...], wr_ref[...], preferred_element_type=F32))
    o_ref[...] = x_ref[...] + gt_ref[...] * y


def _outproj(x, gate, d_out, r_out, w_out, *, seq_len):
    n, d = x.shape
    kw = d_out.shape[1]
    tm = PROJ_ROW_TILE
    row = lambda i: (i, 0)
    return pl.pallas_call(
        _outproj_kernel,
        out_shape=jax.ShapeDtypeStruct((n, d), F32),
        grid=(n // tm,),
        in_specs=[pl.BlockSpec((tm, d), row),
                  pl.BlockSpec((None, 1, d), lambda i: ((i * tm) // seq_len, 0, 0)),
                  pl.BlockSpec((tm, kw), row),
                  pl.BlockSpec((tm, kw), row),
                  pl.BlockSpec((kw, d), lambda i: (0, 0)),
                  pl.BlockSpec((kw, d), lambda i: (1, 0))],
        out_specs=pl.BlockSpec((tm, d), row),
        compiler_params=_params(("parallel",)),
        name="outproj",
    )(x, gate, d_out, r_out, w_out, w_out)


def _rotary_tables(seq_len):
    pos = jnp.arange(seq_len, dtype=F32)
    inv = ROPE_BASE ** (-jnp.arange(0, HEAD_DIM, 2, dtype=F32) / HEAD_DIM)
    ang = pos[:, None] * inv[None, :]
    cos, sin = jnp.cos(ang), jnp.sin(ang)
    return jnp.concatenate([cos, cos], axis=1), jnp.concatenate([-sin, sin], axis=1)


def _trunk(x, mod, w, bias_tiles):
    batch, seq_len, d = x.shape
    x = x.reshape(batch * seq_len, d)
    sh1, sc1, g1, shm, scm, gm, sh2, sc2, g2 = [mod[:, i].reshape(batch, 1, d) for i in range(N_MOD)]
    lam_init = 0.8 - 0.6 * math.exp(-0.3 * 0)
    cos_t, sin_t = _rotary_tables(seq_len)

    x = _ffn(x, sh1, sc1, g1, w["ffn1_norm_g"], w["final_norm_g"], w["ffn1_w13"], w["ffn1_w2"],
             seq_len=seq_len, final_norm=False)
    proj = _inproj(x, shm, scm, w["mix_norm_g"], cos_t, sin_t, w["w_in"], seq_len=seq_len)
    d_out = _diff_attention(proj, bias_tiles, w["lam_params"], w["diff_head_g"],
                            batch=batch, seq_len=seq_len, lam_init=lam_init)
    r_out = _retention(proj, w["ret_decay_fwd"], w["ret_decay_bwd"], w["ret_head_g"],
                       batch=batch, seq_len=seq_len)
    x = _outproj(x, gm, d_out, r_out, w["w_out"], seq_len=seq_len)
    x = _ffn(x, sh2, sc2, g2, w["ffn2_norm_g"], w["final_norm_g"], w["ffn2_w13"], w["ffn2_w2"],
             seq_len=seq_len, final_norm=True)
    return x.reshape(batch, seq_len, d)


def kernel(x_prompt, x_sample, c_prompt, c_sample, ada_w, ada_b, ffn1_norm_g, ffn1_w13, ffn1_w2, mix_norm_g, w_in, diff_lambda_q1, diff_lambda_k1, diff_lambda_q2, diff_lambda_k2, diff_head_g, rel_bias, ret_decay_fwd, ret_decay_bwd, ret_head_g, w_out, ffn2_norm_g, ffn2_w13, ffn2_w2, final_norm_g):
    assert ada_w.shape[0] == 1, "single-layer trunk"
    d = x_prompt.shape[-1]
    nb = c_prompt.shape[0]
    row = lambda a: a.reshape(1, -1)
    w = {
        "ffn1_norm_g": ffn1_norm_g, "mix_norm_g": mix_norm_g, "ffn2_norm_g": ffn2_norm_g,
        "final_norm_g": row(final_norm_g), "diff_head_g": diff_head_g, "ret_head_g": ret_head_g,
        "ffn1_w13": ffn1_w13[0].astype(BF16), "ffn1_w2": ffn1_w2[0].astype(BF16),
        "ffn2_w13": ffn2_w13[0].astype(BF16), "ffn2_w2": ffn2_w2[0].astype(BF16),
        "w_in": w_in[0].astype(BF16), "w_out": w_out[0].astype(BF16),
        "ret_decay_fwd": ret_decay_fwd, "ret_decay_bwd": ret_decay_bwd,
        "lam_params": jnp.concatenate([diff_lambda_q1, diff_lambda_k1, diff_lambda_q2, diff_lambda_k2], axis=0),
    }
    mod = _ada_mod(jnp.concatenate([c_prompt, c_sample], axis=0), ada_w[0], ada_b[0])
    mod = mod.reshape(mod.shape[0], N_MOD, d)
    bias_tiles = _bias_tiles(rel_bias, BIAS_TILE)
    y_prompt = _trunk(x_prompt, mod[:nb], w, bias_tiles)
    y_sample = _trunk(x_sample, mod[nb:], w, bias_tiles)
    return (y_prompt, y_sample)
```

```python
import functools
import math

import jax
import jax.numpy as jnp
from jax import lax
from jax.experimental import pallas as pl
from jax.experimental.pallas import tpu as pltpu

F32 = jnp.float32
BF16 = jnp.bfloat16

HEAD_DIM = 128
DIFF_QK_DIM = HEAD_DIM // 2
N_BUCKETS = 32
REL_MAX_DIST = 128
ROPE_BASE = 10000.0
N_MOD = 9
EPS = 1e-6
LOG2E = math.log2(math.e)
N_PROJ = 7

V7X_VMEM_BYTES = 64 * 1024 * 1024
VMEM_LIMIT = 56 * 1024 * 1024

FFN_ROW_TILE = 512
FFN_HID_TILE = 512
INPROJ_ROW_TILE = 256
PROJ_ROW_TILE = 512
NORM_ROW_BLOCK = 64
NORM_UNROLL = 2
ATTN_TILE = 1024
BIAS_TILE = 512
ATTN_MAX_KEY_TILES = 8
ATTN_BOUND_SLACK = 1.0
ATTN_MIN_DENOM = 2.0 ** -60
SUM_ROWS = 16
BIAS_REACH = 2
RET_CHUNK = 256
RET_UNROLL = 16
ADA_COL_TILE = 1024


def _params(sem):
    return pltpu.CompilerParams(dimension_semantics=sem, vmem_limit_bytes=VMEM_LIMIT)


def _rms_rows(x, g):
    return x * lax.rsqrt(jnp.mean(x * x, axis=-1, keepdims=True) + EPS) * g


def _silu(x):
    return x * jax.nn.sigmoid(x)


def _norm_modulate(x_ref, ng_ref, sc_ref, sh_ref, h_ref):
    gain = ng_ref[...] * (1.0 + sc_ref[...])
    shift = sh_ref[...]

    def block(i, carry):
        rows = pl.ds(pl.multiple_of(i * NORM_ROW_BLOCK, NORM_ROW_BLOCK), NORM_ROW_BLOCK)
        x = x_ref[rows, :]
        r = lax.rsqrt(jnp.mean(x * x, axis=-1, keepdims=True) + EPS)
        h_ref[rows, :] = (x * r * gain + shift).astype(BF16)
        return carry

    lax.fori_loop(0, x_ref.shape[0] // NORM_ROW_BLOCK, block, 0, unroll=NORM_UNROLL)


def _ada_kernel(c_ref, w_ref, b_ref, o_ref):
    a = _silu(c_ref[...]).astype(BF16)
    o_ref[...] = jnp.dot(a, w_ref[...].astype(BF16), preferred_element_type=F32) + b_ref[...]


def _ada_mod(c, w, b):
    nb, d = c.shape
    n = w.shape[1]
    tn = ADA_COL_TILE
    return pl.pallas_call(
        _ada_kernel,
        out_shape=jax.ShapeDtypeStruct((nb, n), F32),
        grid=(n // tn,),
        in_specs=[pl.BlockSpec((nb, d), lambda j: (0, 0)),
                  pl.BlockSpec((d, tn), lambda j: (0, j)),
                  pl.BlockSpec((1, tn), lambda j: (0, j))],
        out_specs=pl.BlockSpec((nb, tn), lambda j: (0, j)),
        compiler_params=_params(("parallel",)),
        name="ada_mod",
    )(c, w, b.reshape(1, n))


def _ffn_kernel(x_ref, sh_ref, sc_ref, gt_ref, ng_ref, fg_ref, w1_ref, w3_ref, w2_ref,
                o_ref, h_ref, *, final_norm):
    j = pl.program_id(1)

    @pl.when(j == 0)
    def _():
        _norm_modulate(x_ref, ng_ref, sc_ref, sh_ref, h_ref)
        o_ref[...] = jnp.zeros(o_ref.shape, F32)

    h = h_ref[...]
    g = jnp.dot(h, w1_ref[...], preferred_element_type=F32)
    u = jnp.dot(h, w3_ref[...], preferred_element_type=F32)
    o_ref[...] += jnp.dot((_silu(g) * u).astype(BF16), w2_ref[...], preferred_element_type=F32)

    @pl.when(j == pl.num_programs(1) - 1)
    def _():
        y = x_ref[...] + 0.5 * gt_ref[...] * o_ref[...]
        if final_norm:
            y = _rms_rows(y, fg_ref[...])
        o_ref[...] = y


def _ffn(x, shift, scale, gate, norm_g, final_g, w13, w2, *, seq_len, final_norm):
    n, d = x.shape
    tm, tf = FFN_ROW_TILE, FFN_HID_TILE
    nf = w2.shape[0] // tf
    seq = lambda i, j: ((i * tm) // seq_len, 0, 0)
    row = lambda i, j: (i, 0)
    const = lambda i, j: (0, 0)
    return pl.pallas_call(
        functools.partial(_ffn_kernel, final_norm=final_norm),
        out_shape=jax.ShapeDtypeStruct((n, d), F32),
        grid=(n // tm, nf),
        in_specs=[pl.BlockSpec((tm, d), row),
                  pl.BlockSpec((None, 1, d), seq),
                  pl.BlockSpec((None, 1, d), seq),
                  pl.BlockSpec((None, 1, d), seq),
                  pl.BlockSpec((1, d), const),
                  pl.BlockSpec((1, d), const),
                  pl.BlockSpec((d, tf), lambda i, j: (0, j)),
                  pl.BlockSpec((d, tf), lambda i, j: (0, nf + j)),
                  pl.BlockSpec((tf, d), lambda i, j: (j, 0))],
        out_specs=pl.BlockSpec((tm, d), row),
        scratch_shapes=[pltpu.VMEM((tm, d), BF16)],
        compiler_params=_params(("parallel", "arbitrary")),
        name="ffn_final" if final_norm else "ffn",
    )(x, shift, scale, gate, norm_g, final_g, w13, w13, w2)


def _inproj_kernel(x_ref, sh_ref, sc_ref, ng_ref, cq_ref, sq_ref, w_ref, o_ref, h_ref):
    _norm_modulate(x_ref, ng_ref, sc_ref, sh_ref, h_ref)
    h = h_ref[...]
    cos, sin = cq_ref[...], sq_ref[...]
    gw = w_ref.shape[1] // N_PROJ
    for j in range(N_PROJ):
        res = jnp.dot(h, w_ref[:, j * gw:(j + 1) * gw], preferred_element_type=F32)
        if j == 0:
            o_ref[j] = (res * (DIFF_QK_DIM ** -0.5 * LOG2E)).astype(BF16)
        elif j in (3, 4):
            scale = 1.0 if j == 3 else HEAD_DIM ** -0.5
            for hd in range(res.shape[1] // HEAD_DIM):
                xs = res[:, hd * HEAD_DIM:(hd + 1) * HEAD_DIM]
                rot = pltpu.roll(xs, HEAD_DIM // 2, 1)
                o_ref[j, :, hd * HEAD_DIM:(hd + 1) * HEAD_DIM] = ((xs * cos + rot * sin) * scale).astype(BF16)
        else:
            o_ref[j] = res.astype(BF16)


def _inproj(x, shift, scale, norm_g, cos_t, sin_t, w_in, *, seq_len):
    n, d = x.shape
    gw = w_in.shape[1] // N_PROJ
    tm = INPROJ_ROW_TILE
    spt = seq_len // tm
    seq = lambda i: ((i * tm) // seq_len, 0, 0)
    return pl.pallas_call(
        _inproj_kernel,
        out_shape=jax.ShapeDtypeStruct((N_PROJ, n, gw), BF16),
        grid=(n // tm,),
        in_specs=[pl.BlockSpec((tm, d), lambda i: (i, 0)),
                  pl.BlockSpec((None, 1, d), seq),
                  pl.BlockSpec((None, 1, d), seq),
                  pl.BlockSpec((1, d), lambda i: (0, 0)),
                  pl.BlockSpec((tm, HEAD_DIM), lambda i: (i % spt, 0)),
                  pl.BlockSpec((tm, HEAD_DIM), lambda i: (i % spt, 0)),
                  pl.BlockSpec(w_in.shape, lambda i: (0, 0), pipeline_mode=pl.Buffered(1))],
        out_specs=pl.BlockSpec((N_PROJ, tm, gw), lambda i: (0, i, 0)),
        scratch_shapes=[pltpu.VMEM((tm, d), BF16)],
        compiler_params=_params(("parallel",)),
        name="inproj",
    )(x, shift, scale, norm_g, cos_t, sin_t, w_in)


def _t5_bucket(rel):
    half = N_BUCKETS // 2
    max_exact = half // 2
    ret = jnp.where(rel > 0, half, 0)
    n = jnp.abs(rel)
    nf = jnp.maximum(n, 1).astype(F32)
    large = max_exact + (jnp.log(nf / max_exact) / math.log(REL_MAX_DIST / max_exact)
                         * (half - max_exact)).astype(jnp.int32)
    large = jnp.minimum(large, half - 1)
    return ret + jnp.where(n < max_exact, n, large)


def _bias_kernel(rb_ref, bk_ref, o_ref):
    h = pl.program_id(0)
    half = N_BUCKETS // 2
    for i in range(o_ref.shape[0]):
        d = i - BIAS_REACH
        if abs(d) > 1:
            b = half - 1 if d < 0 else N_BUCKETS - 1
            o_ref[i] = jnp.full(o_ref.shape[1:], rb_ref[b, h] * LOG2E, F32)
            continue
        bk = bk_ref[d + 1]
        acc = jnp.zeros(bk.shape, F32)
        for b in (range(half) if d < 0 else range(half, N_BUCKETS) if d > 0 else range(N_BUCKETS)):
            acc = jnp.where(bk == b, rb_ref[b, h] * LOG2E, acc)
        o_ref[i] = acc


def _bias_tiles(rel_bias, t):
    assert t >= REL_MAX_DIST
    nh = rel_bias.shape[1]
    i = jnp.arange(t, dtype=jnp.int32)
    rel0 = i[:, None] - i[None, :]
    nt = 2 * BIAS_REACH + 1
    buckets = jnp.stack([_t5_bucket(rel0 + d * t) for d in (-1, 0, 1)])
    return pl.pallas_call(
        _bias_kernel,
        out_shape=jax.ShapeDtypeStruct((nh, nt, t, t), F32),
        grid=(nh,),
        in_specs=[pl.BlockSpec(memory_space=pltpu.SMEM),
                  pl.BlockSpec((3, t, t), lambda h: (0, 0, 0))],
        out_specs=pl.BlockSpec((None, nt, t, t), lambda h: (h, 0, 0, 0)),
        compiler_params=_params(("parallel",)),
        name="bias_tiles",
    )(rel_bias, buckets)


def _attn_kernel(lamp_ref, q_ref, k_ref, v_ref, bias_ref, g_ref, o_ref,
                 vt_ref, qt_ref, u_ref, m_ref, acc_ref, *, t, seq_len, lam_init):
    nk = seq_len // t
    d = HEAD_DIM
    assert nk <= ATTN_MAX_KEY_TILES

    row = lax.broadcasted_iota(jnp.int32, (d, t), 0)
    eye = (lax.broadcasted_iota(jnp.int32, (d, d), 0) == lax.broadcasted_iota(jnp.int32, (d, d), 1)).astype(BF16)
    transpose = lambda x: lax.dot_general(eye, x, (((1,), (1,)), ((), ())), preferred_element_type=F32)
    ksq = jnp.zeros((1, 1), F32)
    for c in range(nk):
        cols = slice(c * t, (c + 1) * t)
        vt_ref[:d, cols] = transpose(v_ref[cols, :]).astype(BF16)
        qt = transpose(q_ref[cols, :])
        for idx, qmask in enumerate((row < DIFF_QK_DIM, row >= DIFF_QK_DIM)):
            qh = jnp.where(qmask, qt, 0.0)
            qt_ref[idx, :, cols] = qh.astype(BF16)
            u_ref[idx, :, cols] = jnp.sum(qh * qh, axis=0, keepdims=True)
        kn = jnp.sum(jnp.square(k_ref[cols, :].astype(F32)), axis=1, keepdims=True)
        ksq = jnp.maximum(ksq, jnp.max(kn, axis=0, keepdims=True))
    vt_ref[d:, :] = jnp.ones((SUM_ROWS, seq_len), BF16)
    center = bias_ref[BIAS_REACH]
    bmax = jnp.max(jnp.max(center, axis=0, keepdims=True), axis=1, keepdims=True)
    u_ref[...] = jnp.sqrt(u_ref[...] * ksq) + (bmax + ATTN_BOUND_SLACK)

    lp = lamp_ref[...]
    lam = (jnp.exp(jnp.sum(lp[0:1] * lp[1:2], axis=1, keepdims=True))
           - jnp.exp(jnp.sum(lp[2:3] * lp[3:4], axis=1, keepdims=True)) + lam_init)

    def tile(i):
        return pl.ds(i * t, t) if isinstance(i, int) else pl.ds(pl.multiple_of(i * t, t), t)

    def tile_operands(qi, j):
        kt = k_ref[tile(j), :]
        r = t // BIAS_TILE
        bias = jnp.concatenate([jnp.concatenate(
            [bias_ref[jnp.clip(r * (j - qi) + a - b, -BIAS_REACH, BIAS_REACH) + BIAS_REACH] for b in range(r)],
            axis=1) for a in range(r)], axis=0)
        return kt, bias

    def queries(ref, idx, qi):
        return ref[idx, :, pl.ds(pl.multiple_of(qi * t, t), t)]

    def values(j):
        return vt_ref[:, tile(j)]

    def bounded_tile(qi):
        def accumulate(j, ps):
            vt = values(j)
            for idx in range(2):
                pv = jnp.dot(vt, ps[idx], preferred_element_type=F32)
                acc_ref[idx] = pv if j == 0 else acc_ref[idx] + pv

        pending = None
        for j in range(nk):
            kt, bias = tile_operands(qi, j)
            ps = []
            for idx in range(2):
                s = jnp.dot(kt, queries(qt_ref, idx, qi), preferred_element_type=F32)
                ps.append(jnp.exp2(s + bias - queries(u_ref, idx, qi)).astype(BF16))
            if pending is not None:
                accumulate(*pending)
            pending = (j, ps)
        accumulate(*pending)

    def running_max_tile(qi):
        m_ref[...] = jnp.full(m_ref.shape, -jnp.inf, F32)
        acc_ref[...] = jnp.zeros(acc_ref.shape, F32)

        def key_tile(j, c):
            kt, bias = tile_operands(qi, j)
            vt = values(j)
            for idx in range(2):
                s = jnp.dot(kt, queries(qt_ref, idx, qi), preferred_element_type=F32) + bias
                m_old = m_ref[idx]
                m_new = jnp.maximum(m_old, jnp.max(s, axis=0, keepdims=True))
                p = jnp.exp2(s - m_new).astype(BF16)
                m_ref[idx] = m_new
                acc_ref[idx] = jnp.exp2(m_old - m_new) * acc_ref[idx] + jnp.dot(vt, p, preferred_element_type=F32)
            return c

        lax.fori_loop(0, nk, key_tile, 0)

    def query_tile(qi, carry):
        bounded_tile(qi)
        denom = jnp.minimum(acc_ref[0, d:d + 1], acc_ref[1, d:d + 1])

        @pl.when(jnp.logical_not(jnp.min(denom) >= ATTN_MIN_DENOM))
        def _():
            running_max_tile(qi)

        out = acc_ref[0, :d] / acc_ref[0, d:d + 1] - lam * (acc_ref[1, :d] / acc_ref[1, d:d + 1])
        y = out * lax.rsqrt(jnp.mean(out * out, axis=0, keepdims=True) + EPS)
        o_ref[pl.ds(pl.multiple_of(qi * t, t), t), :] = (y.T * g_ref[...] * (1.0 - lam_init)).astype(BF16)
        return carry

    lax.fori_loop(0, nk, query_tile, 0)


def _diff_attention(proj, bias_tiles, lam_params, head_g, *, batch, seq_len, lam_init):
    n = proj.shape[1]
    nh = proj.shape[2] // HEAD_DIM
    t = ATTN_TILE
    nt = bias_tiles.shape[1]
    blk = lambda g: pl.BlockSpec((None, seq_len, HEAD_DIM), lambda h, b: (g, b, h))
    return pl.pallas_call(
        functools.partial(_attn_kernel, t=t, seq_len=seq_len, lam_init=lam_init),
        out_shape=jax.ShapeDtypeStruct((n, nh * HEAD_DIM), BF16),
        grid=(nh, batch),
        in_specs=[pl.BlockSpec(lam_params.shape, lambda h, b: (0, 0)),
                  blk(0), blk(1), blk(2),
                  pl.BlockSpec((None, nt, BIAS_TILE, BIAS_TILE), lambda h, b: (h, 0, 0, 0)),
                  pl.BlockSpec((1, HEAD_DIM), lambda h, b: (0, 0))],
        out_specs=pl.BlockSpec((seq_len, HEAD_DIM), lambda h, b: (b, h)),
        scratch_shapes=[pltpu.VMEM((HEAD_DIM + SUM_ROWS, seq_len), BF16),
                        pltpu.VMEM((2, HEAD_DIM, seq_len), BF16),
                        pltpu.VMEM((2, 1, seq_len), F32),
                        pltpu.VMEM((2, 1, t), F32),
                        pltpu.VMEM((2, HEAD_DIM + SUM_ROWS, t), F32)],
        compiler_params=_params(("parallel", "parallel")),
        name="diff_attention",
    )(lam_params, proj, proj, proj, bias_tiles, head_g)


def _log_sigmoid(x):
    return jnp.minimum(x, 0.0) - jnp.log1p(jnp.exp(-jnp.abs(x)))


def _ret_kernel(df_ref, db_ref, q_ref, k_ref, v_ref, rg_ref, g_ref, o_ref, u_ref, r_ref, *, seq_len, c):
    h = pl.program_id(1)
    n = seq_len // c
    d = HEAD_DIM
    df, db = df_ref[0, h], db_ref[0, h]

    lgf = _log_sigmoid(jnp.full((c, d), df, F32))
    lgb = _log_sigmoid(jnp.full((c, d), db, F32))
    ri = lax.broadcasted_iota(jnp.int32, (c, d), 0).astype(F32)
    xi_f, ze_f = jnp.exp(lgf * (ri + 1.0)), jnp.exp(lgf * (c - 1.0 - ri))
    xi_b, ze_b = jnp.exp(lgb * (c - ri)), jnp.exp(lgb * ri)
    gc_f = jnp.exp(_log_sigmoid(jnp.full((d, d), df, F32)) * c)
    gc_b = jnp.exp(_log_sigmoid(jnp.full((d, d), db, F32)) * c)

    diff = (lax.broadcasted_iota(jnp.int32, (c, c), 0) - lax.broadcasted_iota(jnp.int32, (c, c), 1)).astype(F32)
    lgf2 = _log_sigmoid(jnp.full((c, c), df, F32))
    lgb2 = _log_sigmoid(jnp.full((c, c), db, F32))
    decay = jnp.where(diff >= 0, jnp.exp(lgf2 * jnp.maximum(diff, 0.0)), jnp.exp(lgb2 * jnp.maximum(-diff, 0.0)))

    def chunk(ref, i):
        return ref[pl.ds(pl.multiple_of(i * c, c), c), :]

    def local_state(i, carry):
        kc = chunk(k_ref, i).astype(F32)
        kz = jnp.concatenate([kc * ze_f, kc * ze_b], axis=1).astype(BF16)
        u_ref[i] = lax.dot_general(kz, chunk(v_ref, i), (((0,), (0,)), ((), ())), preferred_element_type=F32)
        return carry

    lax.fori_loop(0, n, local_state, 0, unroll=min(n, RET_UNROLL))

    def scan_fwd(i, r):
        r_ref[i, :d, :] = r.astype(BF16)
        return gc_f * r + u_ref[i, :d, :]

    def scan_bwd(i, r):
        i = n - 1 - i
        r_ref[i, d:, :] = r.astype(BF16)
        return gc_b * r + u_ref[i, d:, :]

    lax.fori_loop(0, n, scan_fwd, jnp.zeros((d, d), F32))
    lax.fori_loop(0, n, scan_bwd, jnp.zeros((d, d), F32))

    def outputs(i, carry):
        qb = chunk(q_ref, i)
        qc = qb.astype(F32)
        vc = chunk(v_ref, i)
        qx = jnp.concatenate([qc * xi_f, qc * xi_b], axis=1).astype(BF16)
        inter = jnp.dot(qx, r_ref[i], preferred_element_type=F32)
        inner = lax.dot_general(qb, chunk(k_ref, i), (((1,), (1,)), ((), ())), preferred_element_type=F32) * decay
        o = inter + jnp.dot(inner.astype(BF16), vc, preferred_element_type=F32)
        rg = chunk(rg_ref, i).astype(F32)
        o_ref[pl.ds(pl.multiple_of(i * c, c), c), :] = (_rms_rows(o, g_ref[...]) * _silu(rg)).astype(BF16)
        return carry

    lax.fori_loop(0, n, outputs, 0, unroll=min(n, RET_UNROLL))


def _retention(proj, decay_f, decay_b, head_g, *, batch, seq_len):
    n = proj.shape[1]
    nh = proj.shape[2] // HEAD_DIM
    c = RET_CHUNK
    blk = lambda g: pl.BlockSpec((None, seq_len, HEAD_DIM), lambda b, h: (g, b, h))
    smem = pl.BlockSpec(memory_space=pltpu.SMEM)
    return pl.pallas_call(
        functools.partial(_ret_kernel, seq_len=seq_len, c=c),
        out_shape=jax.ShapeDtypeStruct((n, nh * HEAD_DIM), BF16),
        grid=(batch, nh),
        in_specs=[smem, smem, blk(3), blk(4), blk(5), blk(6),
                  pl.BlockSpec((1, HEAD_DIM), lambda b, h: (0, 0))],
        out_specs=pl.BlockSpec((seq_len, HEAD_DIM), lambda b, h: (b, h)),
        scratch_shapes=[pltpu.VMEM((seq_len // c, 2 * HEAD_DIM, HEAD_DIM), F32),
                        pltpu.VMEM((seq_len // c, 2 * HEAD_DIM, HEAD_DIM), BF16)],
        compiler_params=_params(("parallel", "parallel")),
        name="retention",
    )(decay_f, decay_b, proj, proj, proj, proj, head_g)


def _outproj_kernel(x_ref, gt_ref, a_ref, r_ref, wa_ref, wr_ref, o_ref):
    mix = jnp.concatenate([a_ref[...], r_ref[...]], axis=1)
    w = jnp.concatenate([wa_ref[...], wr_ref[...]], axis=0)
    o_ref[...] = x_ref[...] + gt_ref[...] * jnp.dot(mix, w, preferred_element_type=F32)


def _outproj(x, gate, d_out, r_out, w_out, *, seq_len):
    n, d = x.shape
    kw = d_out.shape[1]
    tm = PROJ_ROW_TILE
    row = lambda i: (i, 0)
    return pl.pallas_call(
        _outproj_kernel,
        out_shape=jax.ShapeDtypeStruct((n, d), F32),
        grid=(n // tm,),
        in_specs=[pl.BlockSpec((tm, d), row),
                  pl.BlockSpec((None, 1, d), lambda i: ((i * tm) // seq_len, 0, 0)),
                  pl.BlockSpec((tm, kw), row),
                  pl.BlockSpec((tm, kw), row),
                  pl.BlockSpec((kw, d), lambda i: (0, 0)),
                  pl.BlockSpec((kw, d), lambda i: (1, 0))],
        out_specs=pl.BlockSpec((tm, d), row),
        compiler_params=_params(("parallel",)),
        name="outproj",
    )(x, gate, d_out, r_out, w_out, w_out)


def _rotary_tables(seq_len):
    pos = jnp.arange(seq_len, dtype=F32)
    inv = ROPE_BASE ** (-jnp.arange(0, HEAD_DIM, 2, dtype=F32) / HEAD_DIM)
    ang = pos[:, None] * inv[None, :]
    cos, sin = jnp.cos(ang), jnp.sin(ang)
    return jnp.concatenate([cos, cos], axis=1), jnp.concatenate([-sin, sin], axis=1)


def _trunk(x, mod, w, bias_tiles):
    batch, seq_len, d = x.shape
    x = x.reshape(batch * seq_len, d)
    sh1, sc1, g1, shm, scm, gm, sh2, sc2, g2 = [mod[:, i].reshape(batch, 1, d) for i in range(N_MOD)]
    lam_init = 0.8 - 0.6 * math.exp(-0.3 * 0)
    cos_t, sin_t = _rotary_tables(seq_len)

    x = _ffn(x, sh1, sc1, g1, w["ffn1_norm_g"], w["final_norm_g"], w["ffn1_w13"], w["ffn1_w2"],
             seq_len=seq_len, final_norm=False)
    proj = _inproj(x, shm, scm, w["mix_norm_g"], cos_t, sin_t, w["w_in"], seq_len=seq_len)
    d_out = _diff_attention(proj, bias_tiles, w["lam_params"], w["diff_head_g"],
                            batch=batch, seq_len=seq_len, lam_init=lam_init)
    r_out = _retention(proj, w["ret_decay_fwd"], w["ret_decay_bwd"], w["ret_head_g"],
                       batch=batch, seq_len=seq_len)
    x = _outproj(x, gm, d_out, r_out, w["w_out"], seq_len=seq_len)
    x = _ffn(x, sh2, sc2, g2, w["ffn2_norm_g"], w["final_norm_g"], w["ffn2_w13"], w["ffn2_w2"],
             seq_len=seq_len, final_norm=True)
    return x.reshape(batch, seq_len, d)


def kernel(x_prompt, x_sample, c_prompt, c_sample, ada_w, ada_b, ffn1_norm_g, ffn1_w13, ffn1_w2, mix_norm_g, w_in, diff_lambda_q1, diff_lambda_k1, diff_lambda_q2, diff_lambda_k2, diff_head_g, rel_bias, ret_decay_fwd, ret_decay_bwd, ret_head_g, w_out, ffn2_norm_g, ffn2_w13, ffn2_w2, final_norm_g):
    assert ada_w.shape[0] == 1, "single-layer trunk"
    d = x_prompt.shape[-1]
    nb = c_prompt.shape[0]
    row = lambda a: a.reshape(1, -1)
    w = {
        "ffn1_norm_g": ffn1_norm_g, "mix_norm_g": mix_norm_g, "ffn2_norm_g": ffn2_norm_g,
        "final_norm_g": row(final_norm_g), "diff_head_g": diff_head_g, "ret_head_g": ret_head_g,
        "ffn1_w13": ffn1_w13[0].astype(BF16), "ffn1_w2": ffn1_w2[0].astype(BF16),
        "ffn2_w13": ffn2_w13[0].astype(BF16), "ffn2_w2": ffn2_w2[0].astype(BF16),
        "w_in": w_in[0].astype(BF16), "w_out": w_out[0].astype(BF16),
        "ret_decay_fwd": ret_decay_fwd, "ret_decay_bwd": ret_decay_bwd,
        "lam_params": jnp.concatenate([diff_lambda_q1, diff_lambda_k1, diff_lambda_q2, diff_lambda_k2], axis=0),
    }
    mod = _ada_mod(jnp.concatenate([c_prompt, c_sample], axis=0), ada_w[0], ada_b[0])
    mod = mod.reshape(mod.shape[0], N_MOD, d)
    bias_tiles = _bias_tiles(rel_bias, BIAS_TILE)
    y_prompt = _trunk(x_prompt, mod[:nb], w, bias_tiles)
    y_sample = _trunk(x_sample, mod[nb:], w, bias_tiles)
    return (y_prompt, y_sample)
```
